```python
import jax, jax.numpy as jnp
from jax import lax
import numpy as np

D_MODEL = 4096
BATCH = 4
SEQ = 4096
DEPTH = 1

MIX_WIDTH = D_MODEL
HEAD_DIM = 128
A_WIDTH = MIX_WIDTH // 2
A_HEADS = A_WIDTH // HEAD_DIM
A_KEY_DIM = 128
A_KEY_WIDTH = A_HEADS * A_KEY_DIM
B_WIDTH = MIX_WIDTH - A_WIDTH
B_GROUP_DIM = 128
B_GROUPS = B_WIDTH // B_GROUP_DIM
GMLP_CHUNK = 128
GLA_CHUNK = 64
IN_COLS = 2 * A_KEY_WIDTH + 2 * A_WIDTH + 2 * B_WIDTH
D_FF = ((8 * D_MODEL + 3 * 256 - 1) // (3 * 256)) * 256
PLE_DIM = 256
EPS = 1e-6

kernel_name = "hymba_hgrn2_gmlp_hybrid"


def _rmsnorm(x, w):
    xf = x.astype(jnp.float32)
    y = xf * lax.rsqrt(jnp.mean(xf * xf, axis=-1, keepdims=True) + EPS)
    return (y * w.astype(jnp.float32)).astype(x.dtype)


def _hgrn2(q, f_pre, i_in, g, lb, norm_w):
    bsz, t, _ = q.shape
    n = t // GLA_CHUNK
    f32 = jnp.float32
    lbf = lb.astype(f32)
    qf = jax.nn.silu(q.astype(f32))
    f = lbf + (1.0 - lbf) * jax.nn.sigmoid(f_pre.astype(f32))
    kf = 1.0 - f
    logf = jnp.log(jnp.maximum(f, 1e-30))

    def to_chunks(a, d):
        return a.reshape(bsz, n, GLA_CHUNK, A_HEADS, d).transpose(1, 0, 3, 2, 4)

    qc = to_chunks(qf, A_KEY_DIM)
    kc = to_chunks(kf, A_KEY_DIM)
    vc = to_chunks(i_in.astype(f32), HEAD_DIM)
    bc = jnp.cumsum(to_chunks(logf, A_KEY_DIM), axis=3)
    causal = jnp.tril(jnp.ones((GLA_CHUNK, GLA_CHUNK), dtype=bool))[:, :, None]

    def step(state, inp):
        q_c, k_c, v_c, b_c = inp
        inter = jnp.einsum('bhtk,bhkv->bhtv', q_c * jnp.exp(b_c), state)
        diff = b_c[:, :, :, None, :] - b_c[:, :, None, :, :]
        decay = jnp.exp(jnp.where(causal, diff, -jnp.inf))
        scores = jnp.einsum('bhtk,bhsk,bhtsk->bhts', q_c, k_c, decay)
        intra = jnp.einsum('bhts,bhsv->bhtv', scores, v_c)
        b_end = b_c[:, :, -1:, :]
        new_state = (jnp.exp(b_end[:, :, 0, :])[..., None] * state
                     + jnp.einsum('bhsk,bhsv->bhkv', k_c * jnp.exp(b_end - b_c), v_c))
        return new_state, inter + intra

    s0 = jnp.zeros((bsz, A_HEADS, A_KEY_DIM, HEAD_DIM), f32)
    _, o = lax.scan(step, s0, (qc, kc, vc, bc))
    o = o.transpose(1, 0, 3, 2, 4).reshape(bsz, t, A_HEADS, HEAD_DIM)
    o = o * lax.rsqrt(jnp.mean(o * o, axis=-1, keepdims=True) + EPS)
    o = o.reshape(bsz, t, A_WIDTH) * norm_w.astype(f32) * jax.nn.silu(g.astype(f32))
    return o.astype(q.dtype)


def _gmlp(u, v, ln_w, ln_b, w_s, b_s):
    bsz, t, _ = u.shape
    n = t // GMLP_CHUNK
    f32 = jnp.float32
    uf = jax.nn.gelu(u.astype(f32), approximate=False)
    vf = jax.nn.gelu(v.astype(f32), approximate=False)
    mu = jnp.mean(vf, axis=-1, keepdims=True)
    var = jnp.mean(jnp.square(vf - mu), axis=-1, keepdims=True)
    vf = (vf - mu) * lax.rsqrt(var + EPS) * ln_w.astype(f32) + ln_b.astype(f32)
    vc = vf.reshape(bsz, n, GMLP_CHUNK, B_GROUPS, B_GROUP_DIM)
    tril = jnp.tril(jnp.ones((GMLP_CHUNK, GMLP_CHUNK), f32))
    w = w_s.astype(f32) * tril
    z = jnp.einsum('gts,bnsgd->bntgd', w, vc) + b_s.astype(f32).T[None, None, :, :, None]
    return (uf * z.reshape(bsz, t, B_WIDTH)).astype(u.dtype)


def setup_inputs(seed: int = 0) -> dict:
    key = jax.random.key(seed)
    ks = jax.random.split(key, 20)
    nrm = jax.random.normal
    f32 = jnp.float32

    def gain(k, shape):
        return 1.0 + 0.01 * nrm(k, shape, f32)

    return {
        "x": nrm(ks[0], (BATCH, SEQ, D_MODEL), f32),
        "p": nrm(ks[1], (DEPTH, BATCH, SEQ, PLE_DIM), f32),
        "pre_mix_w": gain(ks[2], (DEPTH, D_MODEL)),
        "w_in": nrm(ks[3], (DEPTH, D_MODEL, IN_COLS), f32) * D_MODEL ** -0.5,
        "lb_param": nrm(ks[4], (DEPTH + 1, A_KEY_WIDTH), f32) * 0.5,
        "a_norm_w": gain(ks[5], (DEPTH, A_WIDTH)),
        "gmlp_ln_w": gain(ks[6], (DEPTH, B_WIDTH)),
        "gmlp_ln_b": 0.01 * nrm(ks[7], (DEPTH, B_WIDTH), f32),
        "w_spatial": nrm(ks[8], (DEPTH, B_GROUPS, GMLP_CHUNK, GMLP_CHUNK), f32) * GMLP_CHUNK ** -0.5,
        "b_spatial": gain(ks[9], (DEPTH, B_GROUPS, GMLP_CHUNK)),
        "w_out": nrm(ks[10], (DEPTH, MIX_WIDTH, D_MODEL), f32) * MIX_WIDTH ** -0.5,
        "post_mix_w": gain(ks[11], (DEPTH, D_MODEL)),
        "pre_ffn_w": gain(ks[12], (DEPTH, D_MODEL)),
        "w_gate": nrm(ks[13], (DEPTH, D_MODEL, D_FF), f32) * D_MODEL ** -0.5,
        "w_up": nrm(ks[14], (DEPTH, D_MODEL, D_FF), f32) * D_MODEL ** -0.5,
        "w_down": nrm(ks[15], (DEPTH, D_FF, D_MODEL), f32) * D_FF ** -0.5,
        "post_ffn_w": gain(ks[16], (DEPTH, D_MODEL)),
        "w_ple": nrm(ks[17], (DEPTH, PLE_DIM, D_MODEL), f32) * PLE_DIM ** -0.5,
        "w_ple_gate": nrm(ks[18], (DEPTH, D_MODEL, D_MODEL), f32) * D_MODEL ** -0.5,
        "post_ple_w": gain(ks[19], (DEPTH, D_MODEL)),
    }


def reference(x, p, pre_mix_w, w_in, lb_param, a_norm_w, gmlp_ln_w, gmlp_ln_b, w_spatial, b_spatial,
              w_out, post_mix_w, pre_ffn_w, w_gate, w_up, w_down, post_ffn_w, w_ple, w_ple_gate, post_ple_w):
    lower_bounds = jnp.cumsum(jax.nn.softmax(lb_param.astype(jnp.float32), axis=0), axis=0)
    splits = [A_KEY_WIDTH,
              2 * A_KEY_WIDTH,
              2 * A_KEY_WIDTH + A_WIDTH,
              2 * A_KEY_WIDTH + 2 * A_WIDTH,
              2 * A_KEY_WIDTH + 2 * A_WIDTH + B_WIDTH]
    for l in range(DEPTH):
        h = _rmsnorm(x, pre_mix_w[l])
        proj = h @ w_in[l]
        q, f_pre, i_in, g, u, v = jnp.split(proj, splits, axis=-1)
        a_out = _hgrn2(q, f_pre, i_in, g, lower_bounds[l], a_norm_w[l])
        b_out = _gmlp(u, v, gmlp_ln_w[l], gmlp_ln_b[l], w_spatial[l], b_spatial[l])
        mix = jnp.concatenate([a_out, b_out], axis=-1) @ w_out[l]
        x = x + _rmsnorm(mix, post_mix_w[l])
        h = _rmsnorm(x, pre_ffn_w[l])
        ff = (jax.nn.silu(h @ w_gate[l]) * (h @ w_up[l])) @ w_down[l]
        x = x + _rmsnorm(ff, post_ffn_w[l])
        gate = jax.nn.sigmoid(x @ w_ple_gate[l])
        x = x + _rmsnorm((p[l] @ w_ple[l]) * gate, post_ple_w[l])
    return x
```

```python
import functools
import math

import jax
import jax.numpy as jnp
from jax import lax
from jax.experimental import pallas as pl
from jax.experimental.pallas import tpu as pltpu

F32 = jnp.float32
BF16 = jnp.bfloat16
EPS = 1e-6
LANES = 128
HEAD = 128
CHUNK = 64
SUB = 16
NSUB = CHUNK // SUB
VMEM_LIMIT = 56 * 1024 * 1024
NT_DIMS = (((1,), (1,)), ((), ()))
TN_DIMS = (((0,), (0,)), ((), ()))


def _params(sem):
    return pltpu.CompilerParams(dimension_semantics=sem, vmem_limit_bytes=VMEM_LIMIT)


def _sigmoid(x):
    return 1.0 / (1.0 + jnp.exp(-x))


def _gelu(x):
    return 0.5 * x * (1.0 + lax.erf(x * (1.0 / math.sqrt(2.0))))


def _rms(x, w):
    return x * lax.rsqrt(jnp.mean(x * x, axis=-1, keepdims=True) + EPS) * w


def _norm_cast_kernel(x_ref, w_ref, o_ref):
    o_ref[...] = _rms(x_ref[...], w_ref[...]).astype(o_ref.dtype)


def _norm_cast(x, w, tr=256):
    m, d = x.shape
    return pl.pallas_call(
        _norm_cast_kernel,
        grid=(m // tr,),
        in_specs=[pl.BlockSpec((tr, d), lambda i: (i, 0)), pl.BlockSpec((1, d), lambda i: (0, 0))],
        out_specs=pl.BlockSpec((tr, d), lambda i: (i, 0)),
        out_shape=jax.ShapeDtypeStruct((m, d), BF16),
        compiler_params=_params(("parallel",)),
        name="norm_cast",
    )(x, w.reshape(1, d))


def _resid_kernel(x_ref, y_ref, w_ref, *rest, next_mode):
    x_new = x_ref[...] + _rms(y_ref[...].astype(F32), w_ref[...])
    if next_mode == "none":
        (o_ref,) = rest
        o_ref[...] = x_new
    elif next_mode == "cast":
        o_ref, h_ref = rest
        o_ref[...] = x_new
        h_ref[...] = x_new.astype(h_ref.dtype)
    else:
        wn_ref, o_ref, h_ref = rest
        o_ref[...] = x_new
        h_ref[...] = _rms(x_new, wn_ref[...]).astype(h_ref.dtype)


def _resid(x, y, w, w_next=None, next_mode="none", tr=256):
    m, d = x.shape
    row = pl.BlockSpec((tr, d), lambda i: (i, 0))
    vec = pl.BlockSpec((1, d), lambda i: (0, 0))
    in_specs = [row, row, vec]
    args = [x, y, w.reshape(1, d)]
    out_specs = [row]
    out_shape = [jax.ShapeDtypeStruct((m, d), F32)]
    if next_mode == "norm":
        in_specs.append(vec)
        args.append(w_next.reshape(1, d))
    if next_mode != "none":
        out_specs.append(row)
        out_shape.append(jax.ShapeDtypeStruct((m, d), BF16))
    return pl.pallas_call(
        functools.partial(_resid_kernel, next_mode=next_mode),
        grid=(m // tr,),
        in_specs=in_specs,
        out_specs=out_specs,
        out_shape=out_shape,
        compiler_params=_params(("parallel",)),
        name="resid_" + next_mode,
    )(*args)


def _mm_kernel(a_ref, w_ref, o_ref):
    o_ref[...] = jnp.dot(a_ref[...], w_ref[...], preferred_element_type=F32).astype(o_ref.dtype)


def _matmul(a, w, tm=1024, tn=1024):
    m, k = a.shape
    n = w.shape[1]
    return pl.pallas_call(
        _mm_kernel,
        grid=(m // tm, n // tn),
        in_specs=[pl.BlockSpec((tm, k), lambda i, j: (i, 0)), pl.BlockSpec((k, tn), lambda i, j: (0, j))],
        out_specs=pl.BlockSpec((tm, tn), lambda i, j: (i, j)),
        out_shape=jax.ShapeDtypeStruct((m, n), BF16),
        compiler_params=_params(("parallel", "parallel")),
        name="matmul",
    )(a, w)


def _mm2_kernel(a1_ref, a2_ref, w1_ref, w2_ref, o_ref):
    acc = jnp.dot(a1_ref[...], w1_ref[...], preferred_element_type=F32)
    acc = acc + jnp.dot(a2_ref[...], w2_ref[...], preferred_element_type=F32)
    o_ref[...] = acc.astype(o_ref.dtype)


def _matmul_cat(a1, a2, w, tm=1024, tn=1024):
    m, k1 = a1.shape
    k2 = a2.shape[1]
    assert k1 == k2 and w.shape[0] == k1 + k2
    n = w.shape[1]
    return pl.pallas_call(
        _mm2_kernel,
        grid=(m // tm, n // tn),
        in_specs=[pl.BlockSpec((tm, k1), lambda i, j: (i, 0)),
                  pl.BlockSpec((tm, k2), lambda i, j: (i, 0)),
                  pl.BlockSpec((k1, tn), lambda i, j: (0, j)),
                  pl.BlockSpec((k2, tn), lambda i, j: (1, j))],
        out_specs=pl.BlockSpec((tm, tn), lambda i, j: (i, j)),
        out_shape=jax.ShapeDtypeStruct((m, n), BF16),
        compiler_params=_params(("parallel", "parallel")),
        name="matmul_cat",
    )(a1, a2, w, w)


def _mmk_kernel(a_ref, w_ref, o_ref, acc_ref):
    k = pl.program_id(2)
    part = jnp.dot(a_ref[...], w_ref[...], preferred_element_type=F32)

    @pl.when(k == 0)
    def _():
        acc_ref[...] = part

    @pl.when(k > 0)
    def _():
        acc_ref[...] += part

    @pl.when(k == pl.num_programs(2) - 1)
    def _():
        o_ref[...] = acc_ref[...].astype(o_ref.dtype)


def _matmul_ktiled(a, w, tm=1024, tn=1024, tk=2816):
    m, k = a.shape
    n = w.shape[1]
    return pl.pallas_call(
        _mmk_kernel,
        grid=(m // tm, n // tn, k // tk),
        in_specs=[pl.BlockSpec((tm, tk), lambda i, j, kk: (i, kk)),
                  pl.BlockSpec((tk, tn), lambda i, j, kk: (kk, j))],
        out_specs=pl.BlockSpec((tm, tn), lambda i, j, kk: (i, j)),
        out_shape=jax.ShapeDtypeStruct((m, n), BF16),
        scratch_shapes=[pltpu.VMEM((tm, tn), F32)],
        compiler_params=_params(("parallel", "parallel", "arbitrary")),
        name="matmul_ktiled",
    )(a, w)


def _ffn_up_kernel(h_ref, wg_ref, wu_ref, o_ref):
    h = h_ref[...]
    g = jnp.dot(h, wg_ref[...], preferred_element_type=F32)
    u = jnp.dot(h, wu_ref[...], preferred_element_type=F32)
    o_ref[...] = (g * _sigmoid(g) * u).astype(o_ref.dtype)


def _ffn_up(h, wg, wu, tm=1024, tn=512):
    m, k = h.shape
    n = wg.shape[1]
    return pl.pallas_call(
        _ffn_up_kernel,
        grid=(m // tm, n // tn),
        in_specs=[pl.BlockSpec((tm, k), lambda i, j: (i, 0)),
                  pl.BlockSpec((k, tn), lambda i, j: (0, j)),
                  pl.BlockSpec((k, tn), lambda i, j: (0, j))],
        out_specs=pl.BlockSpec((tm, tn), lambda i, j: (i, j)),
        out_shape=jax.ShapeDtypeStruct((m, n), BF16),
        compiler_params=_params(("parallel", "parallel")),
        name="ffn_up",
    )(h, wg, wu)


def _ple_kernel(x_ref, p_ref, wg_ref, wp_ref, o_ref):
    gate = _sigmoid(jnp.dot(x_ref[...], wg_ref[...], preferred_element_type=F32))
    emb = jnp.dot(p_ref[...], wp_ref[...], preferred_element_type=F32)
    o_ref[...] = (emb * gate).astype(o_ref.dtype)


def _ple(xb, pb, wg, wp, tm=1024, tn=1024):
    m, k = xb.shape
    kp = pb.shape[1]
    n = wg.shape[1]
    return pl.pallas_call(
        _ple_kernel,
        grid=(m // tm, n // tn),
        in_specs=[pl.BlockSpec((tm, k), lambda i, j: (i, 0)),
                  pl.BlockSpec((tm, kp), lambda i, j: (i, 0)),
                  pl.BlockSpec((k, tn), lambda i, j: (0, j)),
                  pl.BlockSpec((kp, tn), lambda i, j: (0, j))],
        out_specs=pl.BlockSpec((tm, tn), lambda i, j: (i, j)),
        out_shape=jax.ShapeDtypeStruct((m, n), BF16),
        compiler_params=_params(("parallel", "parallel")),
        name="ple_gate",
    )(xb, pb, wg, wp)


def _hgrn2_kernel(lbp_ref, nw_ref, sel_ref, q_ref, f_ref, i_ref, g_ref, o_ref, st_ref, *, hp, tb):
    @pl.when(pl.program_id(2) == 0)
    def _():
        st_ref[...] = jnp.zeros_like(st_ref)

    lbp = lbp_ref[...]
    ex = jnp.exp(lbp - jnp.max(lbp, axis=0, keepdims=True))
    lb_all = ex[0:1, :] / jnp.sum(ex, axis=0, keepdims=True)
    nw_all = nw_ref[...]

    row = lax.broadcasted_iota(jnp.int32, (CHUNK, CHUNK), 0)
    col = lax.broadcasted_iota(jnp.int32, (CHUNK, CHUNK), 1)
    below_block = (col // SUB) < (row // SUB)
    causal = col <= row
    r3 = lax.broadcasted_iota(jnp.int32, (CHUNK, 3 * CHUNK), 0)
    c3 = lax.broadcasted_iota(jnp.int32, (CHUNK, 3 * CHUNK), 1)
    tril3 = jnp.where((c3 % CHUNK) <= r3, 1.0, 0.0).astype(BF16)
    sel = sel_ref[...]

    def chunk(c, carry):
        r0 = pl.multiple_of(c * CHUNK, CHUNK)
        rows = pl.ds(r0, CHUNK)
        for h in range(hp):
            hs = slice(h * HEAD, (h + 1) * HEAD)
            lb = lb_all[:, hs]
            q = q_ref[rows, hs].astype(F32)
            qf = q * _sigmoid(q)
            f = lb + (1.0 - lb) * _sigmoid(f_ref[rows, hs].astype(F32))
            kf = 1.0 - f
            lf = jnp.log2(jnp.maximum(f, 1e-30))
            hi = lf.astype(BF16)
            r1 = lf - hi.astype(F32)
            mid = r1.astype(BF16)
            lo = (r1 - mid.astype(F32)).astype(BF16)
            b = jnp.dot(tril3, jnp.concatenate([hi, mid, lo], axis=0), preferred_element_type=F32)
            vb = i_ref[rows, hs]
            st = st_ref[h]
            b_end = b[CHUNK - 1:CHUNK, :]
            inter = lax.dot_general((qf * jnp.exp2(b)).astype(BF16), st.astype(BF16), NT_DIMS,
                                    preferred_element_type=F32)
            k_end = (kf * jnp.exp2(b_end - b)).astype(BF16)
            st_ref[h] = st * jnp.exp2(b_end) + lax.dot_general(vb, k_end, TN_DIMS,
                                                               preferred_element_type=F32)
            slabs = [jnp.zeros((SUB, CHUNK), F32)]
            for i in range(1, NSUB):
                m = b[SUB * i - 1:SUB * i, :]
                qt = (qf[SUB * i:SUB * (i + 1)] * jnp.exp2(b[SUB * i:SUB * (i + 1)] - m)).astype(BF16)
                kt = (kf * jnp.exp2(jnp.minimum(m - b, 0.0))).astype(BF16)
                slabs.append(lax.dot_general(qt, kt, NT_DIMS, preferred_element_type=F32))
            off = jnp.concatenate(slabs, axis=0)
            xs = []
            for j in range(NSUB):
                bj = b[SUB * j:SUB * (j + 1)]
                qj = qf[SUB * j:SUB * (j + 1)]
                cols = []
                for s in range(SUB):
                    r = SUB * j + s
                    e = jnp.exp2(jnp.minimum(bj - b[r:r + 1, :], 0.0))
                    cols.append((qj * kf[r:r + 1, :] * e).astype(BF16))
                xs.append(jnp.concatenate(cols, axis=1))
            dg = jnp.dot(jnp.concatenate(xs, axis=0), sel, preferred_element_type=F32)
            scores = jnp.where(below_block, off, jnp.where(causal, dg, 0.0))
            o = inter + jnp.dot(scores.astype(BF16), vb, preferred_element_type=F32)
            o = o * lax.rsqrt(jnp.mean(o * o, axis=-1, keepdims=True) + EPS)
            g = g_ref[rows, hs].astype(F32)
            o_ref[rows, hs] = (o * nw_all[:, hs] * (g * _sigmoid(g))).astype(o_ref.dtype)
        return carry

    lax.fori_loop(0, tb // CHUNK, chunk, 0)


def _hgrn2(proj, lb_param, norm_w, batch, seq, heads, col_blocks, hp=2, tb=512):
    m = proj.shape[0]
    w = hp * HEAD
    nt = seq // tb
    cq, cf, ci, cg = col_blocks
    r = jnp.arange(SUB * HEAD, dtype=jnp.int32)[:, None] // HEAD
    c = jnp.arange(CHUNK, dtype=jnp.int32)[None, :] % SUB
    sel = (r == c).astype(BF16)

    def sec(c0):
        return pl.BlockSpec((tb, w), lambda b, h, t: (b * nt + t, c0 + h))

    return pl.pallas_call(
        functools.partial(_hgrn2_kernel, hp=hp, tb=tb),
        grid=(batch, heads // hp, nt),
        in_specs=[pl.BlockSpec((lb_param.shape[0], w), lambda b, h, t: (0, h)),
                  pl.BlockSpec((1, w), lambda b, h, t: (0, h)),
                  pl.BlockSpec((SUB * HEAD, CHUNK), lambda b, h, t: (0, 0)),
                  sec(cq), sec(cf), sec(ci), sec(cg)],
        out_specs=pl.BlockSpec((tb, w), lambda b, h, t: (b * nt + t, h)),
        out_shape=jax.ShapeDtypeStruct((m, heads * HEAD), BF16),
        scratch_shapes=[pltpu.VMEM((hp, HEAD, HEAD), F32)],
        compiler_params=_params(("parallel", "parallel", "arbitrary")),
        name="hgrn2",
    )(lb_param, norm_w.reshape(1, -1), sel, proj, proj, proj, proj)


def _gmlp_kernel(u_ref, v_ref, lnw_ref, lnb_ref, ws_ref, bs_ref, o_ref, *, groups):
    v = _gelu(v_ref[...].astype(F32))
    mu = jnp.mean(v, axis=-1, keepdims=True)
    d = v - mu
    var = jnp.mean(d * d, axis=-1, keepdims=True)
    vn = (d * lax.rsqrt(var + EPS) * lnw_ref[...] + lnb_ref[...]).astype(BF16)
    row = lax.broadcasted_iota(jnp.int32, (HEAD, HEAD), 0)
    col = lax.broadcasted_iota(jnp.int32, (HEAD, HEAD), 1)
    causal = col <= row
    bs = bs_ref[...]
    for g in range(groups):
        gs = slice(g * HEAD, (g + 1) * HEAD)
        w = jnp.where(causal, ws_ref[g], 0.0).astype(BF16)
        z = jnp.dot(w, vn[:, gs], preferred_element_type=F32) + bs[:, g:g + 1]
        o_ref[:, gs] = (_gelu(u_ref[:, gs].astype(F32)) * z).astype(o_ref.dtype)


def _gmlp(proj, ln_w, ln_b, w_s, b_s, col_u, col_v):
    m = proj.shape[0]
    groups = w_s.shape[0]
    w = groups * HEAD
    return pl.pallas_call(
        functools.partial(_gmlp_kernel, groups=groups),
        grid=(m // HEAD,),
        in_specs=[pl.BlockSpec((HEAD, w), lambda i: (i, col_u)),
                  pl.BlockSpec((HEAD, w), lambda i: (i, col_v)),
                  pl.BlockSpec((1, w), lambda i: (0, 0)),
                  pl.BlockSpec((1, w), lambda i: (0, 0)),
                  pl.BlockSpec((groups, HEAD, HEAD), lambda i: (0, 0, 0)),
                  pl.BlockSpec((HEAD, groups), lambda i: (0, 0))],
        out_specs=pl.BlockSpec((HEAD, w), lambda i: (i, 0)),
        out_shape=jax.ShapeDtypeStruct((m, w), BF16),
        compiler_params=_params(("parallel",)),
        name="gmlp",
    )(proj, proj, ln_w.reshape(1, w), ln_b.reshape(1, w), w_s, b_s.T)


def _layer(x2d, p2d, batch, seq, pre_mix_w, w_in, lb_param, a_norm_w, gmlp_ln_w, gmlp_ln_b, w_spatial,
           b_spatial, w_out, post_mix_w, pre_ffn_w, w_gate, w_up, w_down, post_ffn_w, w_ple, w_ple_gate,
           post_ple_w):
    a_width = a_norm_w.shape[0]
    b_width = gmlp_ln_w.shape[0]
    heads = a_width // HEAD
    assert w_in.shape[1] == 4 * a_width + 2 * b_width and a_width == b_width
    hp = 2
    sec = a_width // (hp * HEAD)

    h = _norm_cast(x2d, pre_mix_w)
    proj = _matmul(h, w_in.astype(BF16))
    a_out = _hgrn2(proj, lb_param, a_norm_w, batch, seq, heads, (0, sec, 2 * sec, 3 * sec), hp=hp)
    b_out = _gmlp(proj, gmlp_ln_w, gmlp_ln_b, w_spatial, b_spatial, 4, 5)
    mix = _matmul_cat(a_out, b_out, w_out.astype(BF16))
    x1, h2 = _resid(x2d, mix, post_mix_w, pre_ffn_w, next_mode="norm")

    d_ff = w_gate.shape[1]
    ff_tile = 1024
    pad = (-d_ff) % ff_tile
    wg = jnp.pad(w_gate.astype(BF16), ((0, 0), (0, pad)))
    wu = jnp.pad(w_up.astype(BF16), ((0, 0), (0, pad)))
    wd = jnp.pad(w_down.astype(BF16), ((0, pad), (0, 0)))
    act = _ffn_up(h2, wg, wu)
    ff = _matmul_ktiled(act, wd, tk=(d_ff + pad) // 4)
    x2, x2b = _resid(x1, ff, post_ffn_w, next_mode="cast")

    y = _ple(x2b, p2d.astype(BF16), w_ple_gate.astype(BF16), w_ple.astype(BF16))
    (out,) = _resid(x2, y, post_ple_w, next_mode="none")
    return out


def kernel(x, p, pre_mix_w, w_in, lb_param, a_norm_w, gmlp_ln_w, gmlp_ln_b, w_spatial, b_spatial, w_out,
           post_mix_w, pre_ffn_w, w_gate, w_up, w_down, post_ffn_w, w_ple, w_ple_gate, post_ple_w):
    batch, seq, d = x.shape
    depth = p.shape[0]
    assert depth == 1 and lb_param.shape[0] == depth + 1
    x2d = x.reshape(batch * seq, d)
    for l in range(depth):
        x2d = _layer(x2d, p[l].reshape(batch * seq, -1), batch, seq, pre_mix_w[l], w_in[l], lb_param,
                     a_norm_w[l], gmlp_ln_w[l], gmlp_ln_b[l], w_spatial[l], b_spatial[l], w_out[l],
                     post_mix_w[l], pre_ffn_w[l], w_gate[l], w_up[l], w_down[l], post_ffn_w[l], w_ple[l],
                     w_ple_gate[l], post_ple_w[l])
    return x2d.reshape(batch, seq, d)
```

```python
import functools
import math

import jax
import jax.numpy as jnp
import numpy as np
from jax import lax
from jax.experimental import pallas as pl
from jax.experimental.pallas import tpu as pltpu

F32 = jnp.float32
BF16 = jnp.bfloat16
EPS = 1e-6
LANES = 128
HEAD = 128
CHUNK = 128
LEVELS = (64, 32, 16, 8, 4, 2, 1)
VMEM_LIMIT = 56 * 1024 * 1024
NT_DIMS = (((1,), (1,)), ((), ()))
TN_DIMS = (((0,), (0,)), ((), ()))


def _params(sem):
    return pltpu.CompilerParams(dimension_semantics=sem, vmem_limit_bytes=VMEM_LIMIT)


def _sigmoid(x):
    return 1.0 / (1.0 + jnp.exp(-x))


def _gelu(x):
    return 0.5 * x * (1.0 + lax.erf(x * (1.0 / math.sqrt(2.0))))


def _rms(x, w):
    return x * lax.rsqrt(jnp.mean(x * x, axis=-1, keepdims=True) + EPS) * w


def _norm_cast_kernel(x_ref, w_ref, o_ref):
    o_ref[...] = _rms(x_ref[...], w_ref[...]).astype(o_ref.dtype)


def _norm_cast(x, w, tr=256):
    m, d = x.shape
    return pl.pallas_call(
        _norm_cast_kernel,
        grid=(m // tr,),
        in_specs=[pl.BlockSpec((tr, d), lambda i: (i, 0)), pl.BlockSpec((1, d), lambda i: (0, 0))],
        out_specs=pl.BlockSpec((tr, d), lambda i: (i, 0)),
        out_shape=jax.ShapeDtypeStruct((m, d), BF16),
        compiler_params=_params(("parallel",)),
        name="norm_cast",
    )(x, w.reshape(1, d))


def _resid_kernel(x_ref, y_ref, w_ref, *rest, next_mode):
    x_new = x_ref[...] + _rms(y_ref[...].astype(F32), w_ref[...])
    if next_mode == "none":
        (o_ref,) = rest
        o_ref[...] = x_new
    elif next_mode == "cast":
        o_ref, h_ref = rest
        o_ref[...] = x_new
        h_ref[...] = x_new.astype(h_ref.dtype)
    else:
        wn_ref, o_ref, h_ref = rest
        o_ref[...] = x_new
        h_ref[...] = _rms(x_new, wn_ref[...]).astype(h_ref.dtype)


def _resid(x, y, w, w_next=None, next_mode="none", tr=256):
    m, d = x.shape
    row = pl.BlockSpec((tr, d), lambda i: (i, 0))
    vec = pl.BlockSpec((1, d), lambda i: (0, 0))
    in_specs = [row, row, vec]
    args = [x, y, w.reshape(1, d)]
    out_specs = [row]
    out_shape = [jax.ShapeDtypeStruct((m, d), F32)]
    if next_mode == "norm":
        in_specs.append(vec)
        args.append(w_next.reshape(1, d))
    if next_mode != "none":
        out_specs.append(row)
        out_shape.append(jax.ShapeDtypeStruct((m, d), BF16))
    return pl.pallas_call(
        functools.partial(_resid_kernel, next_mode=next_mode),
        grid=(m // tr,),
        in_specs=in_specs,
        out_specs=out_specs,
        out_shape=out_shape,
        compiler_params=_params(("parallel",)),
        name="resid_" + next_mode,
    )(*args)


def _mm_kernel(a_ref, w_ref, o_ref):
    o_ref[...] = jnp.dot(a_ref[...], w_ref[...], preferred_element_type=F32).astype(o_ref.dtype)


def _matmul(a, w, tm=1024, tn=1024):
    m, k = a.shape
    n = w.shape[1]
    return pl.pallas_call(
        _mm_kernel,
        grid=(m // tm, n // tn),
        in_specs=[pl.BlockSpec((tm, k), lambda i, j: (i, 0)), pl.BlockSpec((k, tn), lambda i, j: (0, j))],
        out_specs=pl.BlockSpec((tm, tn), lambda i, j: (i, j)),
        out_shape=jax.ShapeDtypeStruct((m, n), BF16),
        compiler_params=_params(("parallel", "parallel")),
        name="matmul",
    )(a, w)


def _mm2_kernel(a1_ref, a2_ref, w1_ref, w2_ref, o_ref):
    acc = jnp.dot(a1_ref[...], w1_ref[...], preferred_element_type=F32)
    acc = acc + jnp.dot(a2_ref[...], w2_ref[...], preferred_element_type=F32)
    o_ref[...] = acc.astype(o_ref.dtype)


def _matmul_cat(a1, a2, w, tm=1024, tn=1024):
    m, k1 = a1.shape
    k2 = a2.shape[1]
    assert k1 == k2 and w.shape[0] == k1 + k2
    n = w.shape[1]
    return pl.pallas_call(
        _mm2_kernel,
        grid=(m // tm, n // tn),
        in_specs=[pl.BlockSpec((tm, k1), lambda i, j: (i, 0)),
                  pl.BlockSpec((tm, k2), lambda i, j: (i, 0)),
                  pl.BlockSpec((k1, tn), lambda i, j: (0, j)),
                  pl.BlockSpec((k2, tn), lambda i, j: (1, j))],
        out_specs=pl.BlockSpec((tm, tn), lambda i, j: (i, j)),
        out_shape=jax.ShapeDtypeStruct((m, n), BF16),
        compiler_params=_params(("parallel", "parallel")),
        name="matmul_cat",
    )(a1, a2, w, w)


def _mmk_kernel(a_ref, w_ref, o_ref, acc_ref):
    k = pl.program_id(2)
    part = jnp.dot(a_ref[...], w_ref[...], preferred_element_type=F32)

    @pl.when(k == 0)
    def _():
        acc_ref[...] = part

    @pl.when(k > 0)
    def _():
        acc_ref[...] += part

    @pl.when(k == pl.num_programs(2) - 1)
    def _():
        o_ref[...] = acc_ref[...].astype(o_ref.dtype)


def _matmul_ktiled(a, w, tm=1024, tn=1024, tk=2816):
    m, k = a.shape
    n = w.shape[1]
    return pl.pallas_call(
        _mmk_kernel,
        grid=(m // tm, n // tn, k // tk),
        in_specs=[pl.BlockSpec((tm, tk), lambda i, j, kk: (i, kk)),
                  pl.BlockSpec((tk, tn), lambda i, j, kk: (kk, j))],
        out_specs=pl.BlockSpec((tm, tn), lambda i, j, kk: (i, j)),
        out_shape=jax.ShapeDtypeStruct((m, n), BF16),
        scratch_shapes=[pltpu.VMEM((tm, tn), F32)],
        compiler_params=_params(("parallel", "parallel", "arbitrary")),
        name="matmul_ktiled",
    )(a, w)


def _ffn_up_kernel(h_ref, wg_ref, wu_ref, o_ref):
    h = h_ref[...]
    g = jnp.dot(h, wg_ref[...], preferred_element_type=F32)
    u = jnp.dot(h, wu_ref[...], preferred_element_type=F32)
    o_ref[...] = (g * _sigmoid(g) * u).astype(o_ref.dtype)


def _ffn_up(h, wg, wu, tm=1024, tn=512):
    m, k = h.shape
    n = wg.shape[1]
    return pl.pallas_call(
        _ffn_up_kernel,
        grid=(m // tm, n // tn),
        in_specs=[pl.BlockSpec((tm, k), lambda i, j: (i, 0)),
                  pl.BlockSpec((k, tn), lambda i, j: (0, j)),
                  pl.BlockSpec((k, tn), lambda i, j: (0, j))],
        out_specs=pl.BlockSpec((tm, tn), lambda i, j: (i, j)),
        out_shape=jax.ShapeDtypeStruct((m, n), BF16),
        compiler_params=_params(("parallel", "parallel")),
        name="ffn_up",
    )(h, wg, wu)


def _ple_kernel(x_ref, p_ref, wg_ref, wp_ref, o_ref):
    gate = _sigmoid(jnp.dot(x_ref[...], wg_ref[...], preferred_element_type=F32))
    emb = jnp.dot(p_ref[...], wp_ref[...], preferred_element_type=F32)
    o_ref[...] = (emb * gate).astype(o_ref.dtype)


def _ple(xb, pb, wg, wp, tm=1024, tn=1024):
    m, k = xb.shape
    kp = pb.shape[1]
    n = wg.shape[1]
    return pl.pallas_call(
        _ple_kernel,
        grid=(m // tm, n // tn),
        in_specs=[pl.BlockSpec((tm, k), lambda i, j: (i, 0)),
                  pl.BlockSpec((tm, kp), lambda i, j: (i, 0)),
                  pl.BlockSpec((k, tn), lambda i, j: (0, j)),
                  pl.BlockSpec((kp, tn), lambda i, j: (0, j))],
        out_specs=pl.BlockSpec((tm, tn), lambda i, j: (i, j)),
        out_shape=jax.ShapeDtypeStruct((m, n), BF16),
        compiler_params=_params(("parallel", "parallel")),
        name="ple_gate",
    )(xb, pb, wg, wp)


def _hgrn2_tables():
    t = np.arange(CHUNK)[:, None]
    s = np.arange(CHUNK)[None, :]
    mats = [s <= t, s > t]
    masks = []
    for h in LEVELS:
        blk = t // h
        odd = blk % 2 == 1
        ref = np.where(odd, blk * h - 1, blk * h + h - 1)
        mats.append(np.where(odd, (s > ref) & (s <= t), (s > t) & (s <= ref)))
        masks.append(odd & (s // h == blk - 1))
    masks.append(t == s)
    zmat = np.tile(np.concatenate(mats, axis=0).astype(np.float32), (1, 2))
    return jnp.asarray(zmat, BF16), jnp.asarray(np.concatenate(masks, axis=1).astype(np.float32), BF16)


def _hgrn2_kernel(lbp_ref, nw_ref, zmat_ref, mask_ref, q_ref, f_ref, i_ref, g_ref, o_ref, st_ref, *, hp, tb):
    @pl.when(pl.program_id(2) == 0)
    def _():
        st_ref[...] = jnp.zeros_like(st_ref)

    lbp = lbp_ref[...]
    ex = jnp.exp(lbp - jnp.max(lbp, axis=0, keepdims=True))
    lb_all = ex[0:1, :] / jnp.sum(ex, axis=0, keepdims=True)
    nw_all = nw_ref[...]
    n_lv = len(LEVELS)

    def chunk(c, carry):
        r0 = pl.multiple_of(c * CHUNK, CHUNK)
        rows = pl.ds(r0, CHUNK)
        q = q_ref[rows, :].astype(F32)
        qf_all = q * _sigmoid(q)
        f = lb_all + (1.0 - lb_all) * _sigmoid(f_ref[rows, :].astype(F32))
        kf_all = 1.0 - f
        lf = jnp.log2(jnp.maximum(f, 1e-30))
        hi = lf.astype(BF16)
        lo = (lf - hi.astype(F32)).astype(BF16)
        z = jnp.dot(zmat_ref[...], jnp.concatenate([hi, lo], axis=0), preferred_element_type=F32)
        e_all = jnp.exp2(z)
        g = g_ref[rows, :].astype(F32)
        gate_all = nw_all * (g * _sigmoid(g))
        for h in range(hp):
            hs = slice(h * HEAD, (h + 1) * HEAD)
            qf = qf_all[:, hs]
            kf = kf_all[:, hs]
            e = e_all[:, hs]
            e_b = e[0:CHUNK]
            vb = i_ref[rows, hs]
            st = st_ref[h]
            inter = lax.dot_general((qf * e_b).astype(BF16), st.astype(BF16), NT_DIMS,
                                    preferred_element_type=F32)
            k_end = (kf * e[CHUNK:2 * CHUNK]).astype(BF16)
            st_ref[h] = st * e_b[CHUNK - 1:CHUNK, :] + lax.dot_general(vb, k_end, TN_DIMS,
                                                                       preferred_element_type=F32)
            ps = []
            for l in range(n_lv):
                e_l = e[(2 + l) * CHUNK:(3 + l) * CHUNK]
                ps.append(lax.dot_general((qf * e_l).astype(BF16), (kf * e_l).astype(BF16), NT_DIMS,
                                          preferred_element_type=F32).astype(BF16))
            ps.append(lax.dot_general(qf.astype(BF16), kf.astype(BF16), NT_DIMS,
                                      preferred_element_type=F32).astype(BF16))
            scores = jnp.concatenate(ps, axis=1) * mask_ref[...]
            o = inter + jnp.dot(scores, jnp.concatenate([vb] * (n_lv + 1), axis=0),
                                preferred_element_type=F32)
            o = o * lax.rsqrt(jnp.mean(o * o, axis=-1, keepdims=True) + EPS)
            o_ref[rows, hs] = (o * gate_all[:, hs]).astype(o_ref.dtype)
        return carry

    lax.fori_loop(0, tb // CHUNK, chunk, 0)


def _hgrn2(proj, lb_param, norm_w, batch, seq, heads, col_blocks, hp=2, tb=512):
    m = proj.shape[0]
    w = hp * HEAD
    nt = seq // tb
    cq, cf, ci, cg = col_blocks
    zmat, masks = _hgrn2_tables()

    def sec(c0):
        return pl.BlockSpec((tb, w), lambda b, h, t: (b * nt + t, c0 + h))

    return pl.pallas_call(
        functools.partial(_hgrn2_kernel, hp=hp, tb=tb),
        grid=(batch, heads // hp, nt),
        in_specs=[pl.BlockSpec((lb_param.shape[0], w), lambda b, h, t: (0, h)),
                  pl.BlockSpec((1, w), lambda b, h, t: (0, h)),
                  pl.BlockSpec(zmat.shape, lambda b, h, t: (0, 0)),
                  pl.BlockSpec(masks.shape, lambda b, h, t: (0, 0)),
                  sec(cq), sec(cf), sec(ci), sec(cg)],
        out_specs=pl.BlockSpec((tb, w), lambda b, h, t: (b * nt + t, h)),
        out_shape=jax.ShapeDtypeStruct((m, heads * HEAD), BF16),
        scratch_shapes=[pltpu.VMEM((hp, HEAD, HEAD), F32)],
        compiler_params=_params(("parallel", "parallel", "arbitrary")),
        name="hgrn2",
    )(lb_param, norm_w.reshape(1, -1), zmat, masks, proj, proj, proj, proj)


def _gmlp_kernel(u_ref, v_ref, lnw_ref, lnb_ref, ws_ref, bs_ref, o_ref, *, groups):
    v = _gelu(v_ref[...].astype(F32))
    mu = jnp.mean(v, axis=-1, keepdims=True)
    d = v - mu
    var = jnp.mean(d * d, axis=-1, keepdims=True)
    vn = (d * lax.rsqrt(var + EPS) * lnw_ref[...] + lnb_ref[...]).astype(BF16)
    row = lax.broadcasted_iota(jnp.int32, (HEAD, HEAD), 0)
    col = lax.broadcasted_iota(jnp.int32, (HEAD, HEAD), 1)
    causal = col <= row
    bs = bs_ref[...]
    for g in range(groups):
        gs = slice(g * HEAD, (g + 1) * HEAD)
        w = jnp.where(causal, ws_ref[g], 0.0).astype(BF16)
        z = jnp.dot(w, vn[:, gs], preferred_element_type=F32) + bs[:, g:g + 1]
        o_ref[:, gs] = (_gelu(u_ref[:, gs].astype(F32)) * z).astype(o_ref.dtype)


def _gmlp(proj, ln_w, ln_b, w_s, b_s, col_u, col_v):
    m = proj.shape[0]
    groups = w_s.shape[0]
    w = groups * HEAD
    return pl.pallas_call(
        functools.partial(_gmlp_kernel, groups=groups),
        grid=(m // HEAD,),
        in_specs=[pl.BlockSpec((HEAD, w), lambda i: (i, col_u)),
                  pl.BlockSpec((HEAD, w), lambda i: (i, col_v)),
                  pl.BlockSpec((1, w), lambda i: (0, 0)),
                  pl.BlockSpec((1, w), lambda i: (0, 0)),
                  pl.BlockSpec((groups, HEAD, HEAD), lambda i: (0, 0, 0)),
                  pl.BlockSpec((HEAD, groups), lambda i: (0, 0))],
        out_specs=pl.BlockSpec((HEAD, w), lambda i: (i, 0)),
        out_shape=jax.ShapeDtypeStruct((m, w), BF16),
        compiler_params=_params(("parallel",)),
        name="gmlp",
    )(proj, proj, ln_w.reshape(1, w), ln_b.reshape(1, w), w_s, b_s.T)


def _layer(x2d, p2d, batch, seq, pre_mix_w, w_in, lb_param, a_norm_w, gmlp_ln_w, gmlp_ln_b, w_spatial,
           b_spatial, w_out, post_mix_w, pre_ffn_w, w_gate, w_up, w_down, post_ffn_w, w_ple, w_ple_gate,
           post_ple_w):
    a_width = a_norm_w.shape[0]
    b_width = gmlp_ln_w.shape[0]
    heads = a_width // HEAD
    assert w_in.shape[1] == 4 * a_width + 2 * b_width and a_width == b_width
    hp = 4
    sec = a_width // (hp * HEAD)

    h = _norm_cast(x2d, pre_mix_w)
    proj = _matmul(h, w_in.astype(BF16))
    a_out = _hgrn2(proj, lb_param, a_norm_w, batch, seq, heads, (0, sec, 2 * sec, 3 * sec), hp=hp)
    b_out = _gmlp(proj, gmlp_ln_w, gmlp_ln_b, w_spatial, b_spatial, 4, 5)
    mix = _matmul_cat(a_out, b_out, w_out.astype(BF16))
    x1, h2 = _resid(x2d, mix, post_mix_w, pre_ffn_w, next_mode="norm")

    d_ff = w_gate.shape[1]
    ff_tile = 1024
    pad = (-d_ff) % ff_tile
    wg = jnp.pad(w_gate.astype(BF16), ((0, 0), (0, pad)))
    wu = jnp.pad(w_up.astype(BF16), ((0, 0), (0, pad)))
    wd = jnp.pad(w_down.astype(BF16), ((0, pad), (0, 0)))
    act = _ffn_up(h2, wg, wu)
    ff = _matmul_ktiled(act, wd, tk=(d_ff + pad) // 4)
    x2, x2b = _resid(x1, ff, post_ffn_w, next_mode="cast")

    y = _ple(x2b, p2d.astype(BF16), w_ple_gate.astype(BF16), w_ple.astype(BF16))
    (out,) = _resid(x2, y, post_ple_w, next_mode="none")
    return out


def kernel(x, p, pre_mix_w, w_in, lb_param, a_norm_w, gmlp_ln_w, gmlp_ln_b, w_spatial, b_spatial, w_out,
           post_mix_w, pre_ffn_w, w_gate, w_up, w_down, post_ffn_w, w_ple, w_ple_gate, post_ple_w):
    batch, seq, d = x.shape
    depth = p.shape[0]
    assert depth == 1 and lb_param.shape[0] == depth + 1
    x2d = x.reshape(batch * seq, d)
    for l in range(depth):
        x2d = _layer(x2d, p[l].reshape(batch * seq, -1), batch, seq, pre_mix_w[l], w_in[l], lb_param,
                     a_norm_w[l], gmlp_ln_w[l], gmlp_ln_b[l], w_spatial[l], b_spatial[l], w_out[l],
                     post_mix_w[l], pre_ffn_w[l], w_gate[l], w_up[l], w_down[l], post_ffn_w[l], w_ple[l],
                     w_ple_gate[l], post_ple_w[l])
    return x2d.reshape(batch, seq, d)
```

```python
import functools
import math

import jax
import jax.numpy as jnp
import numpy as np
from jax import lax
from jax.experimental import pallas as pl
from jax.experimental.pallas import tpu as pltpu

F32 = jnp.float32
BF16 = jnp.bfloat16
EPS = 1e-6
LANES = 128
HEAD = 128
CHUNK = 128
LEVELS = (64, 32, 16, 8, 4, 2, 1)
PACK = 16
VMEM_LIMIT = 56 * 1024 * 1024
NT_DIMS = (((1,), (1,)), ((), ()))
TN_DIMS = (((0,), (0,)), ((), ()))


def _params(sem):
    return pltpu.CompilerParams(dimension_semantics=sem, vmem_limit_bytes=VMEM_LIMIT)


def _sigmoid(x):
    return 0.5 * jnp.tanh(0.5 * x) + 0.5


def _gelu(x):
    return 0.5 * x * (1.0 + lax.erf(x * (1.0 / math.sqrt(2.0))))


def _rms(x, w):
    return x * lax.rsqrt(jnp.mean(x * x, axis=-1, keepdims=True) + EPS) * w


def _norm_cast_kernel(x_ref, w_ref, o_ref):
    o_ref[...] = _rms(x_ref[...], w_ref[...]).astype(o_ref.dtype)


def _norm_cast(x, w, tr=256):
    m, d = x.shape
    return pl.pallas_call(
        _norm_cast_kernel,
        grid=(m // tr,),
        in_specs=[pl.BlockSpec((tr, d), lambda i: (i, 0)), pl.BlockSpec((1, d), lambda i: (0, 0))],
        out_specs=pl.BlockSpec((tr, d), lambda i: (i, 0)),
        out_shape=jax.ShapeDtypeStruct((m, d), BF16),
        compiler_params=_params(("parallel",)),
        name="norm_cast",
    )(x, w.reshape(1, d))


def _resid_kernel(x_ref, y_ref, w_ref, *rest, next_mode):
    x_new = x_ref[...] + _rms(y_ref[...].astype(F32), w_ref[...])
    if next_mode == "none":
        (o_ref,) = rest
        o_ref[...] = x_new
    elif next_mode == "cast":
        o_ref, h_ref = rest
        o_ref[...] = x_new
        h_ref[...] = x_new.astype(h_ref.dtype)
    else:
        wn_ref, o_ref, h_ref = rest
        o_ref[...] = x_new
        h_ref[...] = _rms(x_new, wn_ref[...]).astype(h_ref.dtype)


def _resid(x, y, w, w_next=None, next_mode="none", tr=256):
    m, d = x.shape
    row = pl.BlockSpec((tr, d), lambda i: (i, 0))
    vec = pl.BlockSpec((1, d), lambda i: (0, 0))
    in_specs = [row, row, vec]
    args = [x, y, w.reshape(1, d)]
    out_specs = [row]
    out_shape = [jax.ShapeDtypeStruct((m, d), F32)]
    if next_mode == "norm":
        in_specs.append(vec)
        args.append(w_next.reshape(1, d))
    if next_mode != "none":
        out_specs.append(row)
        out_shape.append(jax.ShapeDtypeStruct((m, d), BF16))
    return pl.pallas_call(
        functools.partial(_resid_kernel, next_mode=next_mode),
        grid=(m // tr,),
        in_specs=in_specs,
        out_specs=out_specs,
        out_shape=out_shape,
        compiler_params=_params(("parallel",)),
        name="resid_" + next_mode,
    )(*args)


def _side_cast_specs(casts, ni, nj):
    specs, n_blocks = [], []
    for c in casts:
        rows = c.shape[0]
        br = next(b for b in range(PACK, rows + 1, PACK) if rows % b == 0 and rows // b <= ni * nj)
        nb = rows // br
        n_blocks.append(nb)
        specs.append(pl.BlockSpec((br, c.shape[1]), lambda i, j, nb=nb: (jnp.minimum(i * nj + j, nb - 1), 0)))
    return specs, tuple(n_blocks)


def _side_cast_step(src_refs, dst_refs, n_blocks):
    step = pl.program_id(0) * pl.num_programs(1) + pl.program_id(1)
    for s_ref, d_ref, nb in zip(src_refs, dst_refs, n_blocks):
        @pl.when(step < nb)
        def _(s_ref=s_ref, d_ref=d_ref):
            d_ref[...] = s_ref[...].astype(d_ref.dtype)


def _mm_cast_kernel(a_ref, w_ref, *refs, n_blocks):
    n_cast = len(n_blocks)
    o_ref = refs[n_cast]
    o_ref[...] = jnp.dot(a_ref[...], w_ref[...], preferred_element_type=F32).astype(o_ref.dtype)
    _side_cast_step(refs[:n_cast], refs[n_cast + 1:], n_blocks)


def _matmul_and_casts(a, w, casts, tm=1024, tn=1024):
    m, k = a.shape
    n = w.shape[1]
    cast_specs, n_blocks = _side_cast_specs(casts, m // tm, n // tn)
    outs = pl.pallas_call(
        functools.partial(_mm_cast_kernel, n_blocks=n_blocks),
        grid=(m // tm, n // tn),
        in_specs=[pl.BlockSpec((tm, k), lambda i, j: (i, 0)), pl.BlockSpec((k, tn), lambda i, j: (0, j))]
        + cast_specs,
        out_specs=[pl.BlockSpec((tm, tn), lambda i, j: (i, j))] + cast_specs,
        out_shape=[jax.ShapeDtypeStruct((m, n), BF16)] + [jax.ShapeDtypeStruct(c.shape, BF16) for c in casts],
        compiler_params=_params(("arbitrary", "arbitrary")),
        name="matmul_casts",
    )(a, w, *casts)
    return outs[0], outs[1:]


def _mm2_kernel(a1_ref, a2_ref, w1_ref, w2_ref, o_ref):
    acc = jnp.dot(a1_ref[...], w1_ref[...], preferred_element_type=F32)
    acc = acc + jnp.dot(a2_ref[...], w2_ref[...], preferred_element_type=F32)
    o_ref[...] = acc.astype(o_ref.dtype)


def _matmul_cat(a1, a2, w, tm=1024, tn=1024):
    m, k1 = a1.shape
    k2 = a2.shape[1]
    assert k1 == k2 and w.shape[0] == k1 + k2
    n = w.shape[1]
    return pl.pallas_call(
        _mm2_kernel,
        grid=(m // tm, n // tn),
        in_specs=[pl.BlockSpec((tm, k1), lambda i, j: (i, 0)),
                  pl.BlockSpec((tm, k2), lambda i, j: (i, 0)),
                  pl.BlockSpec((k1, tn), lambda i, j: (0, j)),
                  pl.BlockSpec((k2, tn), lambda i, j: (1, j))],
        out_specs=pl.BlockSpec((tm, tn), lambda i, j: (i, j)),
        out_shape=jax.ShapeDtypeStruct((m, n), BF16),
        compiler_params=_params(("parallel", "parallel")),
        name="matmul_cat",
    )(a1, a2, w, w)


def _mm_kernel(a_ref, w_ref, o_ref):
    o_ref[...] = jnp.dot(a_ref[...], w_ref[...], preferred_element_type=F32).astype(o_ref.dtype)


def _matmul(a, w, tm, tn):
    m, k = a.shape
    n = w.shape[1]
    return pl.pallas_call(
        _mm_kernel,
        grid=(m // tm, n // tn),
        in_specs=[pl.BlockSpec((tm, k), lambda i, j: (i, 0)), pl.BlockSpec((k, tn), lambda i, j: (0, j))],
        out_specs=pl.BlockSpec((tm, tn), lambda i, j: (i, j)),
        out_shape=jax.ShapeDtypeStruct((m, n), BF16),
        compiler_params=_params(("parallel", "parallel")),
        name="matmul",
    )(a, w)


def _ffn_up_kernel(h_ref, wg_ref, wu_ref, *refs, n_blocks):
    n_cast = len(n_blocks)
    o_ref = refs[n_cast]
    h = h_ref[...]
    g = jnp.dot(h, wg_ref[...], preferred_element_type=F32)
    u = jnp.dot(h, wu_ref[...], preferred_element_type=F32)
    o_ref[...] = (g * _sigmoid(g) * u).astype(o_ref.dtype)
    _side_cast_step(refs[:n_cast], refs[n_cast + 1:], n_blocks)


def _ffn_up(h, wg, wu, casts, tm=1024, tn=512):
    m, k = h.shape
    n = wg.shape[1]
    nj = pl.cdiv(n, tn)
    cast_specs, n_blocks = _side_cast_specs(casts, m // tm, nj)
    outs = pl.pallas_call(
        functools.partial(_ffn_up_kernel, n_blocks=n_blocks),
        grid=(m // tm, nj),
        in_specs=[pl.BlockSpec((tm, k), lambda i, j: (i, 0)),
                  pl.BlockSpec((k, tn), lambda i, j: (0, j)),
                  pl.BlockSpec((k, tn), lambda i, j: (0, j))] + cast_specs,
        out_specs=[pl.BlockSpec((tm, tn), lambda i, j: (i, j))] + cast_specs,
        out_shape=[jax.ShapeDtypeStruct((m, n), BF16)] + [jax.ShapeDtypeStruct(c.shape, BF16) for c in casts],
        compiler_params=_params(("arbitrary", "arbitrary")),
        name="ffn_up",
    )(h, wg, wu, *casts)
    return outs[0], outs[1:]


def _ple_kernel(x_ref, p_ref, wg_ref, wp_ref, o_ref):
    gate = _sigmoid(jnp.dot(x_ref[...], wg_ref[...], preferred_element_type=F32))
    emb = jnp.dot(p_ref[...], wp_ref[...], preferred_element_type=F32)
    o_ref[...] = (emb * gate).astype(o_ref.dtype)


def _ple(xb, pb, wg, wp, tm=1024, tn=1024):
    m, k = xb.shape
    kp = pb.shape[1]
    n = wg.shape[1]
    return pl.pallas_call(
        _ple_kernel,
        grid=(m // tm, n // tn),
        in_specs=[pl.BlockSpec((tm, k), lambda i, j: (i, 0)),
                  pl.BlockSpec((tm, kp), lambda i, j: (i, 0)),
                  pl.BlockSpec((k, tn), lambda i, j: (0, j)),
                  pl.BlockSpec((kp, tn), lambda i, j: (0, j))],
        out_specs=pl.BlockSpec((tm, tn), lambda i, j: (i, j)),
        out_shape=jax.ShapeDtypeStruct((m, n), BF16),
        compiler_params=_params(("parallel", "parallel")),
        name="ple_gate",
    )(xb, pb, wg, wp)


def _hgrn2_tables():
    t = np.arange(CHUNK)[:, None]
    s = np.arange(CHUNK)[None, :]
    mats = [s <= t, s > t]
    masks = []
    for h in LEVELS:
        blk = t // h
        odd = blk % 2 == 1
        ref = np.where(odd, blk * h - 1, blk * h + h - 1)
        mats.append(np.where(odd, (s > ref) & (s <= t), (s > t) & (s <= ref)))
        masks.append(odd & (s // h == blk - 1))
    masks.append(t == s)
    zmat = np.tile(np.concatenate(mats, axis=0).astype(np.float32), (1, 2))
    return jnp.asarray(zmat, BF16), jnp.asarray(np.concatenate(masks, axis=1).astype(np.float32), BF16)


def _hgrn2_kernel(lbp_ref, nw_ref, zmat_ref, mask_ref, q_ref, f_ref, i_ref, g_ref, o_ref, st_ref, *, hp, tb):
    @pl.when(pl.program_id(2) == 0)
    def _():
        st_ref[...] = jnp.zeros_like(st_ref)

    lbp = lbp_ref[...]
    ex = jnp.exp(lbp - jnp.max(lbp, axis=0, keepdims=True))
    lb_all = ex[0:1, :] / jnp.sum(ex, axis=0, keepdims=True)
    nw_all = nw_ref[...]
    n_lv = len(LEVELS)

    def chunk(c, carry):
        r0 = pl.multiple_of(c * CHUNK, CHUNK)
        rows = pl.ds(r0, CHUNK)
        q = q_ref[rows, :].astype(F32)
        qf_all = q * _sigmoid(q)
        f = lb_all + (1.0 - lb_all) * _sigmoid(f_ref[rows, :].astype(F32))
        kf_all = 1.0 - f
        lf = jnp.log2(jnp.maximum(f, 1e-30))
        hi = lf.astype(BF16)
        lo = (lf - hi.astype(F32)).astype(BF16)
        z = jnp.dot(zmat_ref[...], jnp.concatenate([hi, lo], axis=0), preferred_element_type=F32)
        e_all = jnp.exp2(z)
        g = g_ref[rows, :].astype(F32)
        gate_all = nw_all * (g * _sigmoid(g))
        heads = [slice(h * HEAD, (h + 1) * HEAD) for h in range(hp)]
        n_grp = CHUNK // PACK
        vb = [i_ref[rows, hs] for hs in heads]
        inter = []
        for h, hs in enumerate(heads):
            st = st_ref[h]
            e_b = e_all[0:CHUNK, hs]
            inter.append(lax.dot_general((qf_all[:, hs] * e_b).astype(BF16), st.astype(BF16), NT_DIMS,
                                         preferred_element_type=F32))
            k_end = (kf_all[:, hs] * e_all[CHUNK:2 * CHUNK, hs]).astype(BF16)
            st_ref[h] = st * e_b[CHUNK - 1:CHUNK, :] + lax.dot_general(vb[h], k_end, TN_DIMS,
                                                                       preferred_element_type=F32)
        acc = [[None] * n_grp for _ in heads]
        for l, hl in enumerate(LEVELS + (0,)):
            mask_l = mask_ref[:, l * CHUNK:(l + 1) * CHUNK]
            groups = list(range(n_grp))
            if hl >= PACK:
                groups = [gi for gi in groups if (gi * PACK // hl) % 2 == 1]
                take = lambda a, gs=groups: jnp.concatenate([a[gi * PACK:(gi + 1) * PACK] for gi in gs], axis=0)
                mask_l = take(mask_l)
            for h, hs in enumerate(heads):
                if hl == 0:
                    ql, kl = qf_all[:, hs], kf_all[:, hs]
                else:
                    e_l = e_all[(2 + l) * CHUNK:(3 + l) * CHUNK, hs]
                    kl = kf_all[:, hs] * e_l
                    ql = take(qf_all[:, hs]) * take(e_l) if hl >= PACK else qf_all[:, hs] * e_l
                p = lax.dot_general(ql.astype(BF16), kl.astype(BF16), NT_DIMS,
                                    preferred_element_type=F32).astype(BF16) * mask_l
                for n, gi in enumerate(groups):
                    piece = p[n * PACK:(n + 1) * PACK]
                    acc[h][gi] = piece if acc[h][gi] is None else acc[h][gi] + piece
        outs = [inter[h] + jnp.dot(jnp.concatenate(acc[h], axis=0), vb[h], preferred_element_type=F32)
                for h in range(hp)]
        for h, hs in enumerate(heads):
            o = outs[h]
            o = o * lax.rsqrt(jnp.mean(o * o, axis=-1, keepdims=True) + EPS)
            o_ref[rows, hs] = (o * gate_all[:, hs]).astype(o_ref.dtype)
        return carry

    lax.fori_loop(0, tb // CHUNK, chunk, 0, unroll=True)


def _hgrn2(proj, lb_param, norm_w, batch, seq, heads, col_blocks, hp=2, tb=512):
    m = proj.shape[0]
    w = hp * HEAD
    nt = seq // tb
    cq, cf, ci, cg = col_blocks
    zmat, masks = _hgrn2_tables()

    def sec(c0):
        return pl.BlockSpec((tb, w), lambda b, h, t: (b * nt + t, c0 + h))

    return pl.pallas_call(
        functools.partial(_hgrn2_kernel, hp=hp, tb=tb),
        grid=(batch, heads // hp, nt),
        in_specs=[pl.BlockSpec((lb_param.shape[0], w), lambda b, h, t: (0, h)),
                  pl.BlockSpec((1, w), lambda b, h, t: (0, h)),
                  pl.BlockSpec(zmat.shape, lambda b, h, t: (0, 0)),
                  pl.BlockSpec(masks.shape, lambda b, h, t: (0, 0)),
                  sec(cq), sec(cf), sec(ci), sec(cg)],
        out_specs=pl.BlockSpec((tb, w), lambda b, h, t: (b * nt + t, h)),
        out_shape=jax.ShapeDtypeStruct((m, heads * HEAD), BF16),
        scratch_shapes=[pltpu.VMEM((hp, HEAD, HEAD), F32)],
        compiler_params=_params(("parallel", "parallel", "arbitrary")),
        name="hgrn2",
    )(lb_param, norm_w.reshape(1, -1), zmat, masks, proj, proj, proj, proj)


def _gmlp_kernel(u_ref, v_ref, lnw_ref, lnb_ref, ws_ref, bs_ref, o_ref, *, groups):
    v = _gelu(v_ref[...].astype(F32))
    mu = jnp.mean(v, axis=-1, keepdims=True)
    d = v - mu
    var = jnp.mean(d * d, axis=-1, keepdims=True)
    vn = (d * lax.rsqrt(var + EPS) * lnw_ref[...] + lnb_ref[...]).astype(BF16)
    row = lax.broadcasted_iota(jnp.int32, (HEAD, HEAD), 0)
    col = lax.broadcasted_iota(jnp.int32, (HEAD, HEAD), 1)
    causal = col <= row
    bs = bs_ref[...]
    for g in range(groups):
        gs = slice(g * HEAD, (g + 1) * HEAD)
        w = jnp.where(causal, ws_ref[g], 0.0).astype(BF16)
        z = jnp.dot(w, vn[:, gs], preferred_element_type=F32) + bs[:, g:g + 1]
        o_ref[:, gs] = (_gelu(u_ref[:, gs].astype(F32)) * z).astype(o_ref.dtype)


def _gmlp(proj, ln_w, ln_b, w_s, b_s, col_u, col_v):
    m = proj.shape[0]
    groups = w_s.shape[0]
    w = groups * HEAD
    return pl.pallas_call(
        functools.partial(_gmlp_kernel, groups=groups),
        grid=(m // HEAD,),
        in_specs=[pl.BlockSpec((HEAD, w), lambda i: (i, col_u)),
                  pl.BlockSpec((HEAD, w), lambda i: (i, col_v)),
                  pl.BlockSpec((1, w), lambda i: (0, 0)),
                  pl.BlockSpec((1, w), lambda i: (0, 0)),
                  pl.BlockSpec((groups, HEAD, HEAD), lambda i: (0, 0, 0)),
                  pl.BlockSpec((HEAD, groups), lambda i: (0, 0))],
        out_specs=pl.BlockSpec((HEAD, w), lambda i: (i, 0)),
        out_shape=jax.ShapeDtypeStruct((m, w), BF16),
        compiler_params=_params(("parallel",)),
        name="gmlp",
    )(proj, proj, ln_w.reshape(1, w), ln_b.reshape(1, w), w_s, b_s.T)


def _layer(x2d, p2d, batch, seq, pre_mix_w, w_in, lb_param, a_norm_w, gmlp_ln_w, gmlp_ln_b, w_spatial,
           b_spatial, w_out, post_mix_w, pre_ffn_w, w_gate, w_up, w_down, post_ffn_w, w_ple, w_ple_gate,
           post_ple_w):
    a_width = a_norm_w.shape[0]
    b_width = gmlp_ln_w.shape[0]
    heads = a_width // HEAD
    assert w_in.shape[1] == 4 * a_width + 2 * b_width and a_width == b_width
    hp = 4
    sec = a_width // (hp * HEAD)

    h = _norm_cast(x2d, pre_mix_w)
    proj, (wo, wg, wu) = _matmul_and_casts(h, w_in.astype(BF16), (w_out, w_gate, w_up))
    a_out = _hgrn2(proj, lb_param, a_norm_w, batch, seq, heads, (0, sec, 2 * sec, 3 * sec), hp=hp)
    b_out = _gmlp(proj, gmlp_ln_w, gmlp_ln_b, w_spatial, b_spatial, 4, 5)
    mix = _matmul_cat(a_out, b_out, wo)
    x1, h2 = _resid(x2d, mix, post_mix_w, pre_ffn_w, next_mode="norm")

    act, (wd, wpg) = _ffn_up(h2, wg, wu, (w_down, w_ple_gate))
    ff = _matmul(act, wd, tm=512, tn=512)
    x2, x2b = _resid(x1, ff, post_ffn_w, next_mode="cast")

    y = _ple(x2b, p2d.astype(BF16), wpg, w_ple.astype(BF16))
    (out,) = _resid(x2, y, post_ple_w, next_mode="none")
    return out


def kernel(x, p, pre_mix_w, w_in, lb_param, a_norm_w, gmlp_ln_w, gmlp_ln_b, w_spatial, b_spatial, w_out,
           post_mix_w, pre_ffn_w, w_gate, w_up, w_down, post_ffn_w, w_ple, w_ple_gate, post_ple_w):
    batch, seq, d = x.shape
    depth = p.shape[0]
    assert depth == 1 and lb_param.shape[0] == depth + 1
    x2d = x.reshape(batch * seq, d)
    for l in range(depth):
        x2d = _layer(x2d, p[l].reshape(batch * seq, -1), batch, seq, pre_mix_w[l], w_in[l], lb_param,
                     a_norm_w[l], gmlp_ln_w[l], gmlp_ln_b[l], w_spatial[l], b_spatial[l], w_out[l],
                     post_mix_w[l], pre_ffn_w[l], w_gate[l], w_up[l], w_down[l], post_ffn_w[l], w_ple[l],
                     w_ple_gate[l], post_ple_w[l])
    return x2d.reshape(batch, seq, d)
```

```python
import functools
import math

import jax
import jax.numpy as jnp
import numpy as np
from jax import lax
from jax.experimental import pallas as pl
from jax.experimental.pallas import tpu as pltpu

F32 = jnp.float32
BF16 = jnp.bfloat16
EPS = 1e-6
HEAD = 128
CHUNK = 128
LEVELS = (64, 32, 16, 8, 4, 2, 1)
PACK = 16
VMEM_LIMIT = 56 * 1024 * 1024
NT_DIMS = (((1,), (1,)), ((), ()))
TN_DIMS = (((0,), (0,)), ((), ()))


def _params(sem):
    return pltpu.CompilerParams(dimension_semantics=sem, vmem_limit_bytes=VMEM_LIMIT)


def _sigmoid(x):
    return 0.5 * jnp.tanh(0.5 * x) + 0.5


def _gelu(x):
    return 0.5 * x * (1.0 + lax.erf(x * (1.0 / math.sqrt(2.0))))


def _rms(x, w):
    return x * lax.rsqrt(jnp.mean(x * x, axis=-1, keepdims=True) + EPS) * w


def _resid_kernel(x_ref, y_ref, w_ref, o_ref):
    o_ref[...] = x_ref[...] + _rms(y_ref[...].astype(F32), w_ref[...])


def _resid(x, y, w, tr=256):
    m, d = x.shape
    row = pl.BlockSpec((tr, d), lambda i: (i, 0))
    return pl.pallas_call(
        _resid_kernel,
        grid=(m // tr,),
        in_specs=[row, row, pl.BlockSpec((1, d), lambda i: (0, 0))],
        out_specs=row,
        out_shape=jax.ShapeDtypeStruct((m, d), F32),
        compiler_params=_params(("parallel",)),
        name="resid",
    )(x, y, w.reshape(1, d))


def _side_cast_specs(casts, ni, nj):
    specs, n_blocks = [], []
    for c in casts:
        rows = c.shape[0]
        br = next(b for b in range(PACK, rows + 1, PACK) if rows % b == 0 and rows // b <= ni * nj)
        nb = rows // br
        n_blocks.append(nb)
        specs.append(pl.BlockSpec((br, c.shape[1]), lambda i, j, nb=nb: (jnp.minimum(i * nj + j, nb - 1), 0)))
    return specs, tuple(n_blocks)


def _side_cast_step(src_refs, dst_refs, n_blocks):
    step = pl.program_id(0) * pl.num_programs(1) + pl.program_id(1)
    for s_ref, d_ref, nb in zip(src_refs, dst_refs, n_blocks):
        @pl.when(step < nb)
        def _(s_ref=s_ref, d_ref=d_ref):
            d_ref[...] = s_ref[...].astype(d_ref.dtype)


def _skew_kernel(*refs, n_prep, n_vec, n_w, n_extra, n_side, n_blocks, prep_fn, main_fn, ni, n_slices, rs):
    n_cast = len(n_blocks)
    it = iter(refs)
    take = lambda n: [next(it) for _ in range(n)]
    prep_refs, vec_refs, w_refs, extra_refs, cast_src = (take(n_prep), take(n_vec), take(n_w), take(n_extra),
                                                         take(n_cast))
    (o_ref,), side_refs, cast_dst, (a_scr,) = take(1), take(n_side), take(n_cast), take(1)
    i = pl.program_id(0)
    j = pl.program_id(1)

    def prepare():
        a_slice, side_vals = prep_fn([r[...] for r in prep_refs], [v[...] for v in vec_refs])
        r0 = pl.multiple_of(jnp.minimum(j, n_slices - 1) * rs, rs)
        a_scr[i % 2, pl.ds(r0, rs), :] = a_slice.astype(a_scr.dtype)
        for s_ref, val in zip(side_refs, side_vals):
            s_ref[...] = val

    @pl.when(i == 0)
    def _():
        prepare()

    @pl.when(i > 0)
    def _():
        o_ref[...] = main_fn(a_scr[(i + 1) % 2], w_refs, extra_refs).astype(o_ref.dtype)
        prepare()

    _side_cast_step(cast_src, cast_dst, n_blocks)


def _prep_norm(vals, vecs):
    (x,), (w,) = vals, vecs
    return _rms(x, w), []


def _prep_resid_norm(vals, vecs):
    (x, y), (w, w_next) = vals, vecs
    x_new = x + _rms(y.astype(F32), w)
    return _rms(x_new, w_next), [x_new]


def _prep_resid(vals, vecs):
    (x, y), (w,) = vals, vecs
    x_new = x + _rms(y.astype(F32), w)
    return x_new, [x_new]


def _main_dot(a, w_refs, extra_refs):
    return jnp.dot(a, w_refs[0][...], preferred_element_type=F32)


def _main_swiglu(a, w_refs, extra_refs):
    g = jnp.dot(a, w_refs[0][...], preferred_element_type=F32)
    u = jnp.dot(a, w_refs[1][...], preferred_element_type=F32)
    return g * _sigmoid(g) * u


def _main_gated_embedding(a, w_refs, extra_refs):
    gate = _sigmoid(jnp.dot(a, w_refs[0][...], preferred_element_type=F32))
    emb = jnp.dot(extra_refs[0][...], w_refs[1][...], preferred_element_type=F32)
    return emb * gate


def _skew_matmul(name, prep_ins, vecs, prep_fn, n_side, weights, extras, main_fn, n_out, casts, tm, tn,
                 n_slices):
    m, d = prep_ins[0].shape
    ni, nj = m // tm, pl.cdiv(n_out, tn)
    rs = tm // n_slices
    assert nj >= n_slices and rs % PACK == 0
    last = ni * n_slices - 1
    slice_spec = pl.BlockSpec((rs, d), lambda i, j: (jnp.minimum(i * n_slices + jnp.minimum(j, n_slices - 1),
                                                                 last), 0))
    vec_spec = pl.BlockSpec((1, d), lambda i, j: (0, 0))
    col = lambda i, j: jnp.where(i == 0, 0, j)
    cast_specs, n_blocks = _side_cast_specs(casts, ni + 1, nj)
    outs = pl.pallas_call(
        functools.partial(_skew_kernel, n_prep=len(prep_ins), n_vec=len(vecs), n_w=len(weights),
                          n_extra=len(extras), n_side=n_side, n_blocks=n_blocks, prep_fn=prep_fn,
                          main_fn=main_fn, ni=ni, n_slices=n_slices, rs=rs),
        grid=(ni + 1, nj),
        in_specs=[slice_spec] * len(prep_ins) + [vec_spec] * len(vecs)
        + [pl.BlockSpec((w.shape[0], tn), lambda i, j: (0, col(i, j))) for w in weights]
        + [pl.BlockSpec((tm, e.shape[1]), lambda i, j: (jnp.maximum(i - 1, 0), 0)) for e in extras]
        + cast_specs,
        out_specs=[pl.BlockSpec((tm, tn), lambda i, j: (jnp.maximum(i - 1, 0), col(i, j)))]
        + [slice_spec] * n_side + cast_specs,
        out_shape=[jax.ShapeDtypeStruct((m, n_out), BF16)]
        + [jax.ShapeDtypeStruct((m, d), F32)] * n_side
        + [jax.ShapeDtypeStruct(c.shape, BF16) for c in casts],
        scratch_shapes=[pltpu.VMEM((2, tm, d), BF16)],
        compiler_params=_params(("arbitrary", "arbitrary")),
        name=name,
    )(*prep_ins, *[v.reshape(1, d) for v in vecs], *weights, *extras, *casts)
    return outs[0], outs[1:1 + n_side], outs[1 + n_side:]


def _mm2_kernel(a1_ref, a2_ref, w1_ref, w2_ref, *refs, n_blocks):
    n_cast = len(n_blocks)
    o_ref = refs[n_cast]
    acc = jnp.dot(a1_ref[...], w1_ref[...], preferred_element_type=F32)
    acc = acc + jnp.dot(a2_ref[...], w2_ref[...], preferred_element_type=F32)
    o_ref[...] = acc.astype(o_ref.dtype)
    _side_cast_step(refs[:n_cast], refs[n_cast + 1:], n_blocks)


def _matmul_cat(a1, a2, w, casts, tm=1024, tn=1024):
    m, k1 = a1.shape
    k2 = a2.shape[1]
    assert k1 == k2 and w.shape[0] == k1 + k2
    n = w.shape[1]
    cast_specs, n_blocks = _side_cast_specs(casts, m // tm, n // tn)
    outs = pl.pallas_call(
        functools.partial(_mm2_kernel, n_blocks=n_blocks),
        grid=(m // tm, n // tn),
        in_specs=[pl.BlockSpec((tm, k1), lambda i, j: (i, 0)),
                  pl.BlockSpec((tm, k2), lambda i, j: (i, 0)),
                  pl.BlockSpec((k1, tn), lambda i, j: (0, j)),
                  pl.BlockSpec((k2, tn), lambda i, j: (1, j))] + cast_specs,
        out_specs=[pl.BlockSpec((tm, tn), lambda i, j: (i, j))] + cast_specs,
        out_shape=[jax.ShapeDtypeStruct((m, n), BF16)] + [jax.ShapeDtypeStruct(c.shape, BF16) for c in casts],
        compiler_params=_params(("arbitrary", "arbitrary")),
        name="matmul_cat",
    )(a1, a2, w, w, *casts)
    return outs[0], outs[1:]


def _mm_kernel(a_ref, w_ref, o_ref):
    o_ref[...] = jnp.dot(a_ref[...], w_ref[...], preferred_element_type=F32).astype(o_ref.dtype)


def _matmul(a, w, tm, tn):
    m, k = a.shape
    n = w.shape[1]
    return pl.pallas_call(
        _mm_kernel,
        grid=(m // tm, n // tn),
        in_specs=[pl.BlockSpec((tm, k), lambda i, j: (i, 0)), pl.BlockSpec((k, tn), lambda i, j: (0, j))],
        out_specs=pl.BlockSpec((tm, tn), lambda i, j: (i, j)),
        out_shape=jax.ShapeDtypeStruct((m, n), BF16),
        compiler_params=_params(("parallel", "parallel")),
        name="matmul",
    )(a, w)


def _hgrn2_tables():
    t = np.arange(CHUNK)[:, None]
    s = np.arange(CHUNK)[None, :]
    mats = [s <= t, s > t]
    masks = []
    for h in LEVELS:
        blk = t // h
        odd = blk % 2 == 1
        ref = np.where(odd, blk * h - 1, blk * h + h - 1)
        mats.append(np.where(odd, (s > ref) & (s <= t), (s > t) & (s <= ref)))
        masks.append(odd & (s // h == blk - 1))
    masks.append(t == s)
    zmat = np.tile(np.concatenate(mats, axis=0).astype(np.float32), (1, 2))
    return jnp.asarray(zmat, BF16), jnp.asarray(np.concatenate(masks, axis=1).astype(np.float32), BF16)


def _hgrn2_kernel(lbp_ref, nw_ref, zmat_ref, mask_ref, q_ref, f_ref, i_ref, g_ref, o_ref, st_ref, *, hp, tb):
    @pl.when(pl.program_id(2) == 0)
    def _():
        st_ref[...] = jnp.zeros_like(st_ref)

    lbp = lbp_ref[...]
    ex = jnp.exp(lbp - jnp.max(lbp, axis=0, keepdims=True))
    lb_all = ex[0:1, :] / jnp.sum(ex, axis=0, keepdims=True)
    nw_all = nw_ref[...]
    n_lv = len(LEVELS)

    def front(c):
        rows = pl.ds(c * CHUNK, CHUNK)
        q = q_ref[rows, :].astype(F32)
        qf_all = q * _sigmoid(q)
        f = lb_all + (1.0 - lb_all) * _sigmoid(f_ref[rows, :].astype(F32))
        kf_all = 1.0 - f
        lf = jnp.log2(jnp.maximum(f, 1e-30))
        hi = lf.astype(BF16)
        lo = (lf - hi.astype(F32)).astype(BF16)
        z = jnp.dot(zmat_ref[...], jnp.concatenate([hi, lo], axis=0), preferred_element_type=F32)
        e_all = jnp.exp2(z)
        g = g_ref[rows, :].astype(F32)
        gate_all = nw_all * (g * _sigmoid(g))
        return qf_all, kf_all, e_all, gate_all

    def back(c, prepared):
        rows = pl.ds(c * CHUNK, CHUNK)
        qf_all, kf_all, e_all, gate_all = prepared
        heads = [slice(h * HEAD, (h + 1) * HEAD) for h in range(hp)]
        n_grp = CHUNK // PACK
        vb = [i_ref[rows, hs] for hs in heads]
        inter = []
        for h, hs in enumerate(heads):
            st = st_ref[h]
            e_b = e_all[0:CHUNK, hs]
            inter.append(lax.dot_general((qf_all[:, hs] * e_b).astype(BF16), st.astype(BF16), NT_DIMS,
                                         preferred_element_type=F32))
            k_end = (kf_all[:, hs] * e_all[CHUNK:2 * CHUNK, hs]).astype(BF16)
            st_ref[h] = st * e_b[CHUNK - 1:CHUNK, :] + lax.dot_general(vb[h], k_end, TN_DIMS,
                                                                       preferred_element_type=F32)
        acc = [[None] * n_grp for _ in heads]
        for l, hl in enumerate(LEVELS + (0,)):
            mask_l = mask_ref[:, l * CHUNK:(l + 1) * CHUNK]
            groups = list(range(n_grp))
            if hl >= PACK:
                groups = [gi for gi in groups if (gi * PACK // hl) % 2 == 1]
                take = lambda a, gs=groups: jnp.concatenate([a[gi * PACK:(gi + 1) * PACK] for gi in gs], axis=0)
                mask_l = take(mask_l)
            for h, hs in enumerate(heads):
                if hl == 0:
                    ql, kl = qf_all[:, hs], kf_all[:, hs]
                else:
                    e_l = e_all[(2 + l) * CHUNK:(3 + l) * CHUNK, hs]
                    kl = kf_all[:, hs] * e_l
                    ql = take(qf_all[:, hs]) * take(e_l) if hl >= PACK else qf_all[:, hs] * e_l
                p = lax.dot_general(ql.astype(BF16), kl.astype(BF16), NT_DIMS,
                                    preferred_element_type=F32).astype(BF16) * mask_l
                for n, gi in enumerate(groups):
                    piece = p[n * PACK:(n + 1) * PACK]
                    acc[h][gi] = piece if acc[h][gi] is None else acc[h][gi] + piece
        outs = [inter[h] + jnp.dot(jnp.concatenate(acc[h], axis=0), vb[h], preferred_element_type=F32)
                for h in range(hp)]
        for h, hs in enumerate(heads):
            o = outs[h]
            o = o * lax.rsqrt(jnp.mean(o * o, axis=-1, keepdims=True) + EPS)
            o_ref[rows, hs] = (o * gate_all[:, hs]).astype(o_ref.dtype)

    n_chunks = tb // CHUNK
    prepared = front(0)
    for c in range(n_chunks):
        upcoming = front(c + 1) if c + 1 < n_chunks else None
        back(c, prepared)
        prepared = upcoming


def _hgrn2(proj, lb_param, norm_w, batch, seq, heads, col_blocks, hp=2, tb=512):
    m = proj.shape[0]
    w = hp * HEAD
    nt = seq // tb
    cq, cf, ci, cg = col_blocks
    zmat, masks = _hgrn2_tables()

    def sec(c0):
        return pl.BlockSpec((tb, w), lambda b, h, t: (b * nt + t, c0 + h))

    return pl.pallas_call(
        functools.partial(_hgrn2_kernel, hp=hp, tb=tb),
        grid=(batch, heads // hp, nt),
        in_specs=[pl.BlockSpec((lb_param.shape[0], w), lambda b, h, t: (0, h)),
                  pl.BlockSpec((1, w), lambda b, h, t: (0, h)),
                  pl.BlockSpec(zmat.shape, lambda b, h, t: (0, 0)),
                  pl.BlockSpec(masks.shape, lambda b, h, t: (0, 0)),
                  sec(cq), sec(cf), sec(ci), sec(cg)],
        out_specs=pl.BlockSpec((tb, w), lambda b, h, t: (b * nt + t, h)),
        out_shape=jax.ShapeDtypeStruct((m, heads * HEAD), BF16),
        scratch_shapes=[pltpu.VMEM((hp, HEAD, HEAD), F32)],
        compiler_params=_params(("parallel", "parallel", "arbitrary")),
        name="hgrn2",
    )(lb_param, norm_w.reshape(1, -1), zmat, masks, proj, proj, proj, proj)


def _gmlp_kernel(u_ref, v_ref, lnw_ref, lnb_ref, ws_ref, bs_ref, o_ref, *, groups):
    v = _gelu(v_ref[...].astype(F32))
    mu = jnp.mean(v, axis=-1, keepdims=True)
    d = v - mu
    var = jnp.mean(d * d, axis=-1, keepdims=True)
    vn = (d * lax.rsqrt(var + EPS) * lnw_ref[...] + lnb_ref[...]).astype(BF16)
    row = lax.broadcasted_iota(jnp.int32, (HEAD, HEAD), 0)
    col = lax.broadcasted_iota(jnp.int32, (HEAD, HEAD), 1)
    causal = col <= row
    bs = bs_ref[...]
    for g in range(groups):
        gs = slice(g * HEAD, (g + 1) * HEAD)
        w = jnp.where(causal, ws_ref[g], 0.0).astype(BF16)
        z = jnp.dot(w, vn[:, gs], preferred_element_type=F32) + bs[:, g:g + 1]
        o_ref[:, gs] = (_gelu(u_ref[:, gs].astype(F32)) * z).astype(o_ref.dtype)


def _gmlp(proj, ln_w, ln_b, w_s, b_s, col_u, col_v):
    m = proj.shape[0]
    groups = w_s.shape[0]
    w = groups * HEAD
    return pl.pallas_call(
        functools.partial(_gmlp_kernel, groups=groups),
        grid=(m // HEAD,),
        in_specs=[pl.BlockSpec((HEAD, w), lambda i: (i, col_u)),
                  pl.BlockSpec((HEAD, w), lambda i: (i, col_v)),
                  pl.BlockSpec((1, w), lambda i: (0, 0)),
                  pl.BlockSpec((1, w), lambda i: (0, 0)),
                  pl.BlockSpec((groups, HEAD, HEAD), lambda i: (0, 0, 0)),
                  pl.BlockSpec((HEAD, groups), lambda i: (0, 0))],
        out_specs=pl.BlockSpec((HEAD, w), lambda i: (i, 0)),
        out_shape=jax.ShapeDtypeStruct((m, w), BF16),
        compiler_params=_params(("parallel",)),
        name="gmlp",
    )(proj, proj, ln_w.reshape(1, w), ln_b.reshape(1, w), w_s, b_s.T)


def _layer(x2d, p2d, batch, seq, pre_mix_w, w_in, lb_param, a_norm_w, gmlp_ln_w, gmlp_ln_b, w_spatial,
           b_spatial, w_out, post_mix_w, pre_ffn_w, w_gate, w_up, w_down, post_ffn_w, w_ple, w_ple_gate,
           post_ple_w):
    a_width = a_norm_w.shape[0]
    b_width = gmlp_ln_w.shape[0]
    heads = a_width // HEAD
    assert w_in.shape[1] == 4 * a_width + 2 * b_width and a_width == b_width
    hp = 4
    sec = a_width // (hp * HEAD)

    d = x2d.shape[1]
    proj, _, (wo, wg) = _skew_matmul(
        "proj", [x2d], [pre_mix_w], _prep_norm, 0, [w_in.astype(BF16)], [], _main_dot, w_in.shape[1],
        (w_out, w_gate), tm=1024, tn=1024, n_slices=8)
    a_out = _hgrn2(proj, lb_param, a_norm_w, batch, seq, heads, (0, sec, 2 * sec, 3 * sec), hp=hp)
    b_out = _gmlp(proj, gmlp_ln_w, gmlp_ln_b, w_spatial, b_spatial, 4, 5)
    mix, (wu,) = _matmul_cat(a_out, b_out, wo, (w_up,))
    act, (x1,), (wd, wpg) = _skew_matmul(
        "ffn_up", [x2d, mix], [post_mix_w, pre_ffn_w], _prep_resid_norm, 1, [wg, wu], [], _main_swiglu,
        w_gate.shape[1], (w_down, w_ple_gate), tm=1024, tn=512, n_slices=16)
    ff = _matmul(act, wd, tm=512, tn=512)
    y, (x2,), _ = _skew_matmul(
        "ple", [x1, ff], [post_ffn_w], _prep_resid, 1, [wpg, w_ple.astype(BF16)], [p2d.astype(BF16)],
        _main_gated_embedding, d, (), tm=1024, tn=512, n_slices=8)
    return _resid(x2, y, post_ple_w)


def kernel(x, p, pre_mix_w, w_in, lb_param, a_norm_w, gmlp_ln_w, gmlp_ln_b, w_spatial, b_spatial, w_out,
           post_mix_w, pre_ffn_w, w_gate, w_up, w_down, post_ffn_w, w_ple, w_ple_gate, post_ple_w):
    batch, seq, d = x.shape
    depth = p.shape[0]
    assert depth == 1 and lb_param.shape[0] == depth + 1
    x2d = x.reshape(batch * seq, d)
    for l in range(depth):
        x2d = _layer(x2d, p[l].reshape(batch * seq, -1), batch, seq, pre_mix_w[l], w_in[l], lb_param,
                     a_norm_w[l], gmlp_ln_w[l], gmlp_ln_b[l], w_spatial[l], b_spatial[l], w_out[l],
                     post_mix_w[l], pre_ffn_w[l], w_gate[l], w_up[l], w_down[l], post_ffn_w[l], w_ple[l],
                     w_ple_gate[l], post_ple_w[l])
    return x2d.reshape(batch, seq, d)
```

```python
import functools
import math

import jax
import jax.numpy as jnp
import numpy as np
from jax import lax
from jax.experimental import pallas as pl
from jax.experimental.pallas import tpu as pltpu

F32 = jnp.float32
BF16 = jnp.bfloat16
EPS = 1e-6
HEAD = 128
CHUNK = 128
LEVELS = (64, 32, 16, 8, 4, 2, 1)
PACK = 16
MXU_COLS = 256
VMEM_LIMIT = 56 * 1024 * 1024
NT_DIMS = (((1,), (1,)), ((), ()))
TN_DIMS = (((0,), (0,)), ((), ()))


def _params(sem):
    return pltpu.CompilerParams(dimension_semantics=sem, vmem_limit_bytes=VMEM_LIMIT)


def _sigmoid(x):
    return 0.5 * jnp.tanh(0.5 * x) + 0.5


def _gelu(x):
    return 0.5 * x * (1.0 + lax.erf(x * (1.0 / math.sqrt(2.0))))


def _rms(x, w):
    return x * lax.rsqrt(jnp.mean(x * x, axis=-1, keepdims=True) + EPS) * w


def _resid_kernel(x_ref, y_ref, w_ref, o_ref):
    o_ref[...] = x_ref[...] + _rms(y_ref[...].astype(F32), w_ref[...])


def _resid(x, y, w, tr=256):
    m, d = x.shape
    row = pl.BlockSpec((tr, d), lambda i: (i, 0))
    return pl.pallas_call(
        _resid_kernel,
        grid=(m // tr,),
        in_specs=[row, row, pl.BlockSpec((1, d), lambda i: (0, 0))],
        out_specs=row,
        out_shape=jax.ShapeDtypeStruct((m, d), F32),
        compiler_params=_params(("parallel",)),
        name="resid",
    )(x, y, w.reshape(1, d))


def _side_cast_specs(casts, ni, nj):
    specs, n_blocks = [], []
    for c in casts:
        rows = c.shape[0]
        br = next(b for b in range(PACK, rows + 1, PACK) if rows % b == 0 and rows // b <= ni * nj)
        nb = rows // br
        n_blocks.append(nb)
        specs.append(pl.BlockSpec((br, c.shape[1]), lambda i, j, nb=nb: (jnp.minimum(i * nj + j, nb - 1), 0)))
    return specs, tuple(n_blocks)


def _side_cast_step(src_refs, dst_refs, n_blocks):
    step = pl.program_id(0) * pl.num_programs(1) + pl.program_id(1)
    for s_ref, d_ref, nb in zip(src_refs, dst_refs, n_blocks):
        @pl.when(step < nb)
        def _(s_ref=s_ref, d_ref=d_ref):
            d_ref[...] = s_ref[...].astype(d_ref.dtype)


def _skew_kernel(*refs, n_prep, n_vec, n_w, n_extra, n_side, n_blocks, prep_fn, main_fn, nj, n_out,
                 n_slices, rs):
    n_cast = len(n_blocks)
    it = iter(refs)
    take = lambda n: [next(it) for _ in range(n)]
    prep_refs, vec_refs, w_refs, extra_refs, cast_src = (take(n_prep), take(n_vec), take(n_w), take(n_extra),
                                                         take(n_cast))
    (o_ref,), side_refs, cast_dst, (a_scr,) = take(1), take(n_side), take(n_cast), take(1)
    i = pl.program_id(0)
    j = pl.program_id(1)
    tn = o_ref.shape[1]

    def prepare():
        a_slice, side_vals = prep_fn([r[...] for r in prep_refs], [v[...] for v in vec_refs])
        r0 = pl.multiple_of(jnp.minimum(j, n_slices - 1) * rs, rs)
        a_scr[i % 2, pl.ds(r0, rs), :] = a_slice.astype(a_scr.dtype)
        for s_ref, val in zip(side_refs, side_vals):
            s_ref[...] = val

    @pl.when(i == 0)
    def _():
        prepare()

    def main(c0, c1):
        cs = slice(c0, c1)
        o_ref[:, cs] = main_fn(a_scr[(i + 1) % 2], w_refs, extra_refs, cs).astype(o_ref.dtype)

    n_live = -(-(n_out - (nj - 1) * tn) // MXU_COLS) * MXU_COLS
    if n_live == tn:
        @pl.when(i > 0)
        def _():
            main(0, tn)
    else:
        @pl.when((i > 0) & (j < nj - 1))
        def _():
            main(0, tn)

        @pl.when((i > 0) & (j == nj - 1))
        def _():
            main(0, n_live)

    @pl.when(i > 0)
    def _():
        prepare()

    _side_cast_step(cast_src, cast_dst, n_blocks)


def _prep_norm(vals, vecs):
    (x,), (w,) = vals, vecs
    return _rms(x, w), []


def _prep_resid_norm(vals, vecs):
    (x, y), (w, w_next) = vals, vecs
    x_new = x + _rms(y.astype(F32), w)
    return _rms(x_new, w_next), [x_new]


def _prep_resid(vals, vecs):
    (x, y), (w,) = vals, vecs
    x_new = x + _rms(y.astype(F32), w)
    return x_new, [x_new]


def _main_dot(a, w_refs, extra_refs, cs):
    return jnp.dot(a, w_refs[0][:, cs], preferred_element_type=F32)


def _main_swiglu(a, w_refs, extra_refs, cs):
    g = jnp.dot(a, w_refs[0][:, cs], preferred_element_type=F32)
    u = jnp.dot(a, w_refs[1][:, cs], preferred_element_type=F32)
    return g * _sigmoid(g) * u


def _main_gated_embedding(a, w_refs, extra_refs, cs):
    gate = _sigmoid(jnp.dot(a, w_refs[0][:, cs], preferred_element_type=F32))
    emb = jnp.dot(extra_refs[0][...], w_refs[1][:, cs], preferred_element_type=F32)
    return emb * gate


def _skew_matmul(name, prep_ins, vecs, prep_fn, n_side, weights, extras, main_fn, n_out, casts, tm, tn,
                 n_slices):
    m, d = prep_ins[0].shape
    ni, nj = m // tm, pl.cdiv(n_out, tn)
    rs = tm // n_slices
    assert nj >= n_slices and tn % MXU_COLS == 0 and rs % PACK == 0
    last = ni * n_slices - 1
    slice_spec = pl.BlockSpec((rs, d), lambda i, j: (jnp.minimum(i * n_slices + jnp.minimum(j, n_slices - 1),
                                                                 last), 0))
    vec_spec = pl.BlockSpec((1, d), lambda i, j: (0, 0))
    col = lambda i, j: jnp.where(i == 0, 0, j)
    cast_specs, n_blocks = _side_cast_specs(casts, ni + 1, nj)
    outs = pl.pallas_call(
        functools.partial(_skew_kernel, n_prep=len(prep_ins), n_vec=len(vecs), n_w=len(weights),
                          n_extra=len(extras), n_side=n_side, n_blocks=n_blocks, prep_fn=prep_fn,
                          main_fn=main_fn, nj=nj, n_out=n_out, n_slices=n_slices, rs=rs),
        grid=(ni + 1, nj),
        in_specs=[slice_spec] * len(prep_ins) + [vec_spec] * len(vecs)
        + [pl.BlockSpec((w.shape[0], tn), lambda i, j: (0, col(i, j))) for w in weights]
        + [pl.BlockSpec((tm, e.shape[1]), lambda i, j: (jnp.maximum(i - 1, 0), 0)) for e in extras]
        + cast_specs,
        out_specs=[pl.BlockSpec((tm, tn), lambda i, j: (jnp.maximum(i - 1, 0), col(i, j)))]
        + [slice_spec] * n_side + cast_specs,
        out_shape=[jax.ShapeDtypeStruct((m, n_out), BF16)]
        + [jax.ShapeDtypeStruct((m, d), F32)] * n_side
        + [jax.ShapeDtypeStruct(c.shape, BF16) for c in casts],
        scratch_shapes=[pltpu.VMEM((2, tm, d), BF16)],
        compiler_params=_params(("arbitrary", "arbitrary")),
        name=name,
    )(*prep_ins, *[v.reshape(1, d) for v in vecs], *weights, *extras, *casts)
    return outs[0], outs[1:1 + n_side], outs[1 + n_side:]


def _mm2_kernel(a1_ref, a2_ref, w1_ref, w2_ref, *refs, n_blocks):
    n_cast = len(n_blocks)
    o_ref = refs[n_cast]
    acc = jnp.dot(a1_ref[...], w1_ref[...], preferred_element_type=F32)
    acc = acc + jnp.dot(a2_ref[...], w2_ref[...], preferred_element_type=F32)
    o_ref[...] = acc.astype(o_ref.dtype)
    _side_cast_step(refs[:n_cast], refs[n_cast + 1:], n_blocks)


def _matmul_cat(a1, a2, w, casts, tm=1024, tn=1024):
    m, k1 = a1.shape
    k2 = a2.shape[1]
    assert k1 == k2 and w.shape[0] == k1 + k2
    n = w.shape[1]
    cast_specs, n_blocks = _side_cast_specs(casts, m // tm, n // tn)
    outs = pl.pallas_call(
        functools.partial(_mm2_kernel, n_blocks=n_blocks),
        grid=(m // tm, n // tn),
        in_specs=[pl.BlockSpec((tm, k1), lambda i, j: (i, 0)),
                  pl.BlockSpec((tm, k2), lambda i, j: (i, 0)),
                  pl.BlockSpec((k1, tn), lambda i, j: (0, j)),
                  pl.BlockSpec((k2, tn), lambda i, j: (1, j))] + cast_specs,
        out_specs=[pl.BlockSpec((tm, tn), lambda i, j: (i, j))] + cast_specs,
        out_shape=[jax.ShapeDtypeStruct((m, n), BF16)] + [jax.ShapeDtypeStruct(c.shape, BF16) for c in casts],
        compiler_params=_params(("arbitrary", "arbitrary")),
        name="matmul_cat",
    )(a1, a2, w, w, *casts)
    return outs[0], outs[1:]


def _mm_kernel(a_ref, w_ref, o_ref):
    o_ref[...] = jnp.dot(a_ref[...], w_ref[...], preferred_element_type=F32).astype(o_ref.dtype)


def _matmul(a, w, tm, tn):
    m, k = a.shape
    n = w.shape[1]
    return pl.pallas_call(
        _mm_kernel,
        grid=(m // tm, n // tn),
        in_specs=[pl.BlockSpec((tm, k), lambda i, j: (i, 0)), pl.BlockSpec((k, tn), lambda i, j: (0, j))],
        out_specs=pl.BlockSpec((tm, tn), lambda i, j: (i, j)),
        out_shape=jax.ShapeDtypeStruct((m, n), BF16),
        compiler_params=_params(("parallel", "parallel")),
        name="matmul",
    )(a, w)


def _hgrn2_tables():
    t = np.arange(CHUNK)[:, None]
    s = np.arange(CHUNK)[None, :]
    mats = [s <= t, s > t]
    masks = []
    for h in LEVELS:
        blk = t // h
        odd = blk % 2 == 1
        ref = np.where(odd, blk * h - 1, blk * h + h - 1)
        mats.append(np.where(odd, (s > ref) & (s <= t), (s > t) & (s <= ref)))
        masks.append(odd & (s // h == blk - 1))
    masks.append(t == s)
    zmat = np.tile(np.concatenate(mats, axis=0).astype(np.float32), (1, 2))
    return jnp.asarray(zmat, BF16), jnp.asarray(np.concatenate(masks, axis=1).astype(np.float32), BF16)


def _hgrn2_kernel(lbp_ref, nw_ref, zmat_ref, mask_ref, q_ref, f_ref, i_ref, g_ref, o_ref, st_ref, *, hp, tb):
    @pl.when(pl.program_id(2) == 0)
    def _():
        st_ref[...] = jnp.zeros_like(st_ref)

    lbp = lbp_ref[...]
    ex = jnp.exp(lbp - jnp.max(lbp, axis=0, keepdims=True))
    lb_all = ex[0:1, :] / jnp.sum(ex, axis=0, keepdims=True)
    nw_all = nw_ref[...]
    n_lv = len(LEVELS)

    def front(c):
        rows = pl.ds(c * CHUNK, CHUNK)
        q = q_ref[rows, :].astype(F32)
        qf_all = q * _sigmoid(q)
        f = lb_all + (1.0 - lb_all) * _sigmoid(f_ref[rows, :].astype(F32))
        kf_all = 1.0 - f
        lf = jnp.log2(jnp.maximum(f, 1e-30))
        hi = lf.astype(BF16)
        lo = (lf - hi.astype(F32)).astype(BF16)
        z = jnp.dot(zmat_ref[...], jnp.concatenate([hi, lo], axis=0), preferred_element_type=F32)
        e_all = jnp.exp2(z)
        g = g_ref[rows, :].astype(F32)
        gate_all = nw_all * (g * _sigmoid(g))
        return qf_all, kf_all, e_all, gate_all

    def back(c, prepared):
        rows = pl.ds(c * CHUNK, CHUNK)
        qf_all, kf_all, e_all, gate_all = prepared
        heads = [slice(h * HEAD, (h + 1) * HEAD) for h in range(hp)]
        n_grp = CHUNK // PACK
        vb = [i_ref[rows, hs] for hs in heads]
        inter = []
        for h, hs in enumerate(heads):
            st = st_ref[h]
            e_b = e_all[0:CHUNK, hs]
            inter.append(lax.dot_general((qf_all[:, hs] * e_b).astype(BF16), st.astype(BF16), NT_DIMS,
                                         preferred_element_type=F32))
            k_end = (kf_all[:, hs] * e_all[CHUNK:2 * CHUNK, hs]).astype(BF16)
            st_ref[h] = st * e_b[CHUNK - 1:CHUNK, :] + lax.dot_general(vb[h], k_end, TN_DIMS,
                                                                       preferred_element_type=F32)
        acc = [[None] * n_grp for _ in heads]
        q_b = [qf_all[:, hs].astype(BF16) for hs in heads]
        k_b = [kf_all[:, hs].astype(BF16) for hs in heads]
        for l, hl in enumerate(LEVELS + (0,)):
            mask_l = mask_ref[:, l * CHUNK:(l + 1) * CHUNK]
            groups = list(range(n_grp))
            if hl >= PACK:
                groups = [gi for gi in groups if (gi * PACK // hl) % 2 == 1]
                take = lambda a, gs=groups: jnp.concatenate([a[gi * PACK:(gi + 1) * PACK] for gi in gs], axis=0)
                mask_l = take(mask_l)
            for h, hs in enumerate(heads):
                if hl == 0:
                    ql, kl = q_b[h], k_b[h]
                else:
                    e_l = e_all[(2 + l) * CHUNK:(3 + l) * CHUNK, hs].astype(BF16)
                    kl = k_b[h] * e_l
                    ql = take(q_b[h]) * take(e_l) if hl >= PACK else q_b[h] * e_l
                p = lax.dot_general(ql, kl, NT_DIMS, preferred_element_type=F32).astype(BF16) * mask_l
                for n, gi in enumerate(groups):
                    piece = p[n * PACK:(n + 1) * PACK]
                    acc[h][gi] = piece if acc[h][gi] is None else acc[h][gi] + piece
        outs = [inter[h] + jnp.dot(jnp.concatenate(acc[h], axis=0), vb[h], preferred_element_type=F32)
                for h in range(hp)]
        for h, hs in enumerate(heads):
            o = outs[h]
            o = o * lax.rsqrt(jnp.mean(o * o, axis=-1, keepdims=True) + EPS)
            o_ref[rows, hs] = (o * gate_all[:, hs]).astype(o_ref.dtype)

    n_chunks = tb // CHUNK
    prepared = front(0)
    for c in range(n_chunks):
        upcoming = front(c + 1) if c + 1 < n_chunks else None
        back(c, prepared)
        prepared = upcoming


def _hgrn2(proj, lb_param, norm_w, batch, seq, heads, col_blocks, hp=2, tb=512):
    m = proj.shape[0]
    w = hp * HEAD
    nt = seq // tb
    cq, cf, ci, cg = col_blocks
    zmat, masks = _hgrn2_tables()

    def sec(c0):
        return pl.BlockSpec((tb, w), lambda b, h, t: (b * nt + t, c0 + h))

    return pl.pallas_call(
        functools.partial(_hgrn2_kernel, hp=hp, tb=tb),
        grid=(batch, heads // hp, nt),
        in_specs=[pl.BlockSpec((lb_param.shape[0], w), lambda b, h, t: (0, h)),
                  pl.BlockSpec((1, w), lambda b, h, t: (0, h)),
                  pl.BlockSpec(zmat.shape, lambda b, h, t: (0, 0)),
                  pl.BlockSpec(masks.shape, lambda b, h, t: (0, 0)),
                  sec(cq), sec(cf), sec(ci), sec(cg)],
        out_specs=pl.BlockSpec((tb, w), lambda b, h, t: (b * nt + t, h)),
        out_shape=jax.ShapeDtypeStruct((m, heads * HEAD), BF16),
        scratch_shapes=[pltpu.VMEM((hp, HEAD, HEAD), F32)],
        compiler_params=_params(("parallel", "parallel", "arbitrary")),
        name="hgrn2",
    )(lb_param, norm_w.reshape(1, -1), zmat, masks, proj, proj, proj, proj)


def _gmlp_kernel(u_ref, v_ref, lnw_ref, lnb_ref, ws_ref, bs_ref, o_ref, *, groups):
    v = _gelu(v_ref[...].astype(F32))
    mu = jnp.mean(v, axis=-1, keepdims=True)
    d = v - mu
    var = jnp.mean(d * d, axis=-1, keepdims=True)
    vn = (d * lax.rsqrt(var + EPS) * lnw_ref[...] + lnb_ref[...]).astype(BF16)
    row = lax.broadcasted_iota(jnp.int32, (HEAD, HEAD), 0)
    col = lax.broadcasted_iota(jnp.int32, (HEAD, HEAD), 1)
    causal = col <= row
    bs = bs_ref[...]
    for g in range(groups):
        gs = slice(g * HEAD, (g + 1) * HEAD)
        w = jnp.where(causal, ws_ref[g], 0.0).astype(BF16)
        z = jnp.dot(w, vn[:, gs], preferred_element_type=F32) + bs[:, g:g + 1]
        o_ref[:, gs] = (_gelu(u_ref[:, gs].astype(F32)) * z).astype(o_ref.dtype)


def _gmlp(proj, ln_w, ln_b, w_s, b_s, col_u, col_v):
    m = proj.shape[0]
    groups = w_s.shape[0]
    w = groups * HEAD
    return pl.pallas_call(
        functools.partial(_gmlp_kernel, groups=groups),
        grid=(m // HEAD,),
        in_specs=[pl.BlockSpec((HEAD, w), lambda i: (i, col_u)),
                  pl.BlockSpec((HEAD, w), lambda i: (i, col_v)),
                  pl.BlockSpec((1, w), lambda i: (0, 0)),
                  pl.BlockSpec((1, w), lambda i: (0, 0)),
                  pl.BlockSpec((groups, HEAD, HEAD), lambda i: (0, 0, 0)),
                  pl.BlockSpec((HEAD, groups), lambda i: (0, 0))],
        out_specs=pl.BlockSpec((HEAD, w), lambda i: (i, 0)),
        out_shape=jax.ShapeDtypeStruct((m, w), BF16),
        compiler_params=_params(("parallel",)),
        name="gmlp",
    )(proj, proj, ln_w.reshape(1, w), ln_b.reshape(1, w), w_s, b_s.T)


def _layer(x2d, p2d, batch, seq, pre_mix_w, w_in, lb_param, a_norm_w, gmlp_ln_w, gmlp_ln_b, w_spatial,
           b_spatial, w_out, post_mix_w, pre_ffn_w, w_gate, w_up, w_down, post_ffn_w, w_ple, w_ple_gate,
           post_ple_w):
    a_width = a_norm_w.shape[0]
    b_width = gmlp_ln_w.shape[0]
    heads = a_width // HEAD
    assert w_in.shape[1] == 4 * a_width + 2 * b_width and a_width == b_width
    hp = 8
    sec = a_width // (hp * HEAD)

    d = x2d.shape[1]
    proj, _, (wo, wg) = _skew_matmul(
        "proj", [x2d], [pre_mix_w], _prep_norm, 0, [w_in.astype(BF16)], [], _main_dot, w_in.shape[1],
        (w_out, w_gate), tm=1024, tn=1024, n_slices=8)
    a_out = _hgrn2(proj, lb_param, a_norm_w, batch, seq, heads, (0, sec, 2 * sec, 3 * sec), hp=hp)
    b_out = _gmlp(proj, gmlp_ln_w, gmlp_ln_b, w_spatial, b_spatial, 4, 5)
    mix, (wu,) = _matmul_cat(a_out, b_out, wo, (w_up,))
    act, (x1,), (wd, wpg) = _skew_matmul(
        "ffn_up", [x2d, mix], [post_mix_w, pre_ffn_w], _prep_resid_norm, 1, [wg, wu], [], _main_swiglu,
        w_gate.shape[1], (w_down, w_ple_gate), tm=1024, tn=512, n_slices=16)
    ff = _matmul(act, wd, tm=512, tn=512)
    y, (x2,), _ = _skew_matmul(
        "ple", [x1, ff], [post_ffn_w], _prep_resid, 1, [wpg, w_ple.astype(BF16)], [p2d.astype(BF16)],
        _main_gated_embedding, d, (), tm=1024, tn=512, n_slices=8)
    return _resid(x2, y, post_ple_w)


def kernel(x, p, pre_mix_w, w_in, lb_param, a_norm_w, gmlp_ln_w, gmlp_ln_b, w_spatial, b_spatial, w_out,
           post_mix_w, pre_ffn_w, w_gate, w_up, w_down, post_ffn_w, w_ple, w_ple_gate, post_ple_w):
    batch, seq, d = x.shape
    depth = p.shape[0]
    assert depth == 1 and lb_param.shape[0] == depth + 1
    x2d = x.reshape(batch * seq, d)
    for l in range(depth):
        x2d = _layer(x2d, p[l].reshape(batch * seq, -1), batch, seq, pre_mix_w[l], w_in[l], lb_param,
                     a_norm_w[l], gmlp_ln_w[l], gmlp_ln_b[l], w_spatial[l], b_spatial[l], w_out[l],
                     post_mix_w[l], pre_ffn_w[l], w_gate[l], w_up[l], w_down[l], post_ffn_w[l], w_ple[l],
                     w_ple_gate[l], post_ple_w[l])
    return x2d.reshape(batch, seq, d)
```

```python
import functools
import math

import jax
import jax.numpy as jnp
import numpy as np
from jax import lax
from jax.experimental import pallas as pl
from jax.experimental.pallas import tpu as pltpu

F32 = jnp.float32
BF16 = jnp.bfloat16
EPS = 1e-6
HEAD = 128
CHUNK = 128
LEVELS = (64, 32, 16, 8, 4, 2, 1)
PACK = 16
MXU_COLS = 256
VMEM_LIMIT = 56 * 1024 * 1024
NT_DIMS = (((1,), (1,)), ((), ()))
TN_DIMS = (((0,), (0,)), ((), ()))


def _params(sem):
    return pltpu.CompilerParams(dimension_semantics=sem, vmem_limit_bytes=VMEM_LIMIT)


def _sigmoid(x):
    return 0.5 * jnp.tanh(0.5 * x) + 0.5


def _silu(x):
    h = 0.5 * x
    return h + h * jnp.tanh(h)


def _gelu(x):
    return 0.5 * x * (1.0 + lax.erf(x * (1.0 / math.sqrt(2.0))))


def _rms(x, w):
    return x * lax.rsqrt(jnp.mean(x * x, axis=-1, keepdims=True) + EPS) * w


def _resid_kernel(x_ref, y_ref, w_ref, o_ref):
    o_ref[...] = x_ref[...] + _rms(y_ref[...].astype(F32), w_ref[...])


def _resid(x, y, w, tr=256):
    m, d = x.shape
    row = pl.BlockSpec((tr, d), lambda i: (i, 0))
    return pl.pallas_call(
        _resid_kernel,
        grid=(m // tr,),
        in_specs=[row, row, pl.BlockSpec((1, d), lambda i: (0, 0))],
        out_specs=row,
        out_shape=jax.ShapeDtypeStruct((m, d), F32),
        compiler_params=_params(("parallel",)),
        name="resid",
    )(x, y, w.reshape(1, d))


def _side_cast_specs(casts, ni, nj):
    specs, n_blocks = [], []
    for c in casts:
        rows = c.shape[0]
        br = next(b for b in range(PACK, rows + 1, PACK) if rows % b == 0 and rows // b <= ni * nj)
        nb = rows // br
        n_blocks.append(nb)
        specs.append(pl.BlockSpec((br, c.shape[1]), lambda i, j, nb=nb: (jnp.minimum(i * nj + j, nb - 1), 0)))
    return specs, tuple(n_blocks)


def _side_cast_step(src_refs, dst_refs, n_blocks):
    step = pl.program_id(0) * pl.num_programs(1) + pl.program_id(1)
    for s_ref, d_ref, nb in zip(src_refs, dst_refs, n_blocks):
        @pl.when(step < nb)
        def _(s_ref=s_ref, d_ref=d_ref):
            d_ref[...] = s_ref[...].astype(d_ref.dtype)


def _skew_kernel(*refs, n_prep, n_vec, n_w, n_extra, n_side, n_blocks, prep_fn, main_fn, nj, n_out,
                 n_slices, rs):
    n_cast = len(n_blocks)
    it = iter(refs)
    take = lambda n: [next(it) for _ in range(n)]
    prep_refs, vec_refs, w_refs, extra_refs, cast_src = (take(n_prep), take(n_vec), take(n_w), take(n_extra),
                                                         take(n_cast))
    (o_ref,), side_refs, cast_dst, (a_scr,) = take(1), take(n_side), take(n_cast), take(1)
    i = pl.program_id(0)
    j = pl.program_id(1)
    tn = o_ref.shape[1]

    def prepare():
        a_slice, side_vals = prep_fn([r[...] for r in prep_refs], [v[...] for v in vec_refs])
        r0 = pl.multiple_of(jnp.minimum(j, n_slices - 1) * rs, rs)
        a_scr[i % 2, pl.ds(r0, rs), :] = a_slice.astype(a_scr.dtype)
        for s_ref, val in zip(side_refs, side_vals):
            s_ref[...] = val

    @pl.when(i == 0)
    def _():
        prepare()

    def step(n_cols):
        cs = slice(0, n_cols)
        o_ref[:, cs] = main_fn(a_scr[(i + 1) % 2], w_refs, extra_refs, cs).astype(o_ref.dtype)
        prepare()

    n_live = -(-(n_out - (nj - 1) * tn) // MXU_COLS) * MXU_COLS
    if n_live == tn:
        @pl.when(i > 0)
        def _():
            step(tn)
    else:
        @pl.when((i > 0) & (j < nj - 1))
        def _():
            step(tn)

        @pl.when((i > 0) & (j == nj - 1))
        def _():
            step(n_live)

    _side_cast_step(cast_src, cast_dst, n_blocks)


def _prep_norm(vals, vecs):
    (x,), (w,) = vals, vecs
    return _rms(x, w), []


def _prep_resid_norm(vals, vecs):
    (x, y), (w, w_next) = vals, vecs
    x_new = x + _rms(y.astype(F32), w)
    return _rms(x_new, w_next), [x_new]


def _prep_resid(vals, vecs):
    (x, y), (w,) = vals, vecs
    x_new = x + _rms(y.astype(F32), w)
    return x_new, [x_new]


def _main_dot(a, w_refs, extra_refs, cs):
    return jnp.dot(a, w_refs[0][:, cs], preferred_element_type=F32)


def _main_swiglu(a, w_refs, extra_refs, cs):
    g = jnp.dot(a, w_refs[0][:, cs], preferred_element_type=F32)
    u = jnp.dot(a, w_refs[1][:, cs], preferred_element_type=F32)
    return _silu(g) * u


def _main_gated_embedding(a, w_refs, extra_refs, cs):
    gate = _sigmoid(jnp.dot(a, w_refs[0][:, cs], preferred_element_type=F32))
    emb = jnp.dot(extra_refs[0][...], w_refs[1][:, cs], preferred_element_type=F32)
    return emb * gate


def _skew_matmul(name, prep_ins, vecs, prep_fn, n_side, weights, extras, main_fn, n_out, casts, tm, tn,
                 n_slices):
    m, d = prep_ins[0].shape
    ni, nj = m // tm, pl.cdiv(n_out, tn)
    rs = tm // n_slices
    assert nj >= n_slices and tn % MXU_COLS == 0 and rs % PACK == 0
    last = ni * n_slices - 1
    slice_spec = pl.BlockSpec((rs, d), lambda i, j: (jnp.minimum(i * n_slices + jnp.minimum(j, n_slices - 1),
                                                                 last), 0))
    vec_spec = pl.BlockSpec((1, d), lambda i, j: (0, 0))
    col = lambda i, j: jnp.where(i == 0, 0, j)
    cast_specs, n_blocks = _side_cast_specs(casts, ni + 1, nj)
    outs = pl.pallas_call(
        functools.partial(_skew_kernel, n_prep=len(prep_ins), n_vec=len(vecs), n_w=len(weights),
                          n_extra=len(extras), n_side=n_side, n_blocks=n_blocks, prep_fn=prep_fn,
                          main_fn=main_fn, nj=nj, n_out=n_out, n_slices=n_slices, rs=rs),
        grid=(ni + 1, nj),
        in_specs=[slice_spec] * len(prep_ins) + [vec_spec] * len(vecs)
        + [pl.BlockSpec((w.shape[0], tn), lambda i, j: (0, col(i, j))) for w in weights]
        + [pl.BlockSpec((tm, e.shape[1]), lambda i, j: (jnp.maximum(i - 1, 0), 0)) for e in extras]
        + cast_specs,
        out_specs=[pl.BlockSpec((tm, tn), lambda i, j: (jnp.maximum(i - 1, 0), col(i, j)))]
        + [slice_spec] * n_side + cast_specs,
        out_shape=[jax.ShapeDtypeStruct((m, n_out), BF16)]
        + [jax.ShapeDtypeStruct((m, d), F32)] * n_side
        + [jax.ShapeDtypeStruct(c.shape, BF16) for c in casts],
        scratch_shapes=[pltpu.VMEM((2, tm, d), BF16)],
        compiler_params=_params(("arbitrary", "arbitrary")),
        name=name,
    )(*prep_ins, *[v.reshape(1, d) for v in vecs], *weights, *extras, *casts)
    return outs[0], outs[1:1 + n_side], outs[1 + n_side:]


def _mm2_kernel(a1_ref, a2_ref, w1_ref, w2_ref, *refs, n_blocks):
    n_cast = len(n_blocks)
    o_ref = refs[n_cast]
    acc = jnp.dot(a1_ref[...], w1_ref[...], preferred_element_type=F32)
    acc = acc + jnp.dot(a2_ref[...], w2_ref[...], preferred_element_type=F32)
    o_ref[...] = acc.astype(o_ref.dtype)
    _side_cast_step(refs[:n_cast], refs[n_cast + 1:], n_blocks)


def _matmul_cat(a1, a2, w, casts, tm=1024, tn=1024):
    m, k1 = a1.shape
    k2 = a2.shape[1]
    assert k1 == k2 and w.shape[0] == k1 + k2
    n = w.shape[1]
    cast_specs, n_blocks = _side_cast_specs(casts, m // tm, n // tn)
    outs = pl.pallas_call(
        functools.partial(_mm2_kernel, n_blocks=n_blocks),
        grid=(m // tm, n // tn),
        in_specs=[pl.BlockSpec((tm, k1), lambda i, j: (i, 0)),
                  pl.BlockSpec((tm, k2), lambda i, j: (i, 0)),
                  pl.BlockSpec((k1, tn), lambda i, j: (0, j)),
                  pl.BlockSpec((k2, tn), lambda i, j: (1, j))] + cast_specs,
        out_specs=[pl.BlockSpec((tm, tn), lambda i, j: (i, j))] + cast_specs,
        out_shape=[jax.ShapeDtypeStruct((m, n), BF16)] + [jax.ShapeDtypeStruct(c.shape, BF16) for c in casts],
        compiler_params=_params(("arbitrary", "arbitrary")),
        name="matmul_cat",
    )(a1, a2, w, w, *casts)
    return outs[0], outs[1:]


def _mm_kernel(a_ref, w_ref, o_ref):
    o_ref[...] = jnp.dot(a_ref[...], w_ref[...], preferred_element_type=F32).astype(o_ref.dtype)


def _matmul(a, w, tm, tn):
    m, k = a.shape
    n = w.shape[1]
    return pl.pallas_call(
        _mm_kernel,
        grid=(m // tm, n // tn),
        in_specs=[pl.BlockSpec((tm, k), lambda i, j: (i, 0)), pl.BlockSpec((k, tn), lambda i, j: (0, j))],
        out_specs=pl.BlockSpec((tm, tn), lambda i, j: (i, j)),
        out_shape=jax.ShapeDtypeStruct((m, n), BF16),
        compiler_params=_params(("parallel", "parallel")),
        name="matmul",
    )(a, w)


def _hgrn2_tables():
    t = np.arange(CHUNK)[:, None]
    s = np.arange(CHUNK)[None, :]
    mats = [s <= t, s > t]
    masks = []
    for h in LEVELS:
        blk = t // h
        odd = blk % 2 == 1
        ref = np.where(odd, blk * h - 1, blk * h + h - 1)
        mats.append(np.where(odd, (s > ref) & (s <= t), (s > t) & (s <= ref)))
        masks.append(odd & (s // h == blk - 1))
    masks.append(t == s)
    zmat = np.tile(np.concatenate(mats, axis=0).astype(np.float32), (1, 2))
    return jnp.asarray(zmat, BF16), jnp.asarray(np.concatenate(masks, axis=1).astype(np.float32), BF16)


def _hgrn2_kernel(lbp_ref, nw_ref, zmat_ref, mask_ref, q_ref, f_ref, i_ref, g_ref, o_ref, st_ref, *, hp, tb):
    @pl.when(pl.program_id(2) == 0)
    def _():
        st_ref[...] = jnp.zeros_like(st_ref)

    lbp = lbp_ref[...]
    ex = jnp.exp(lbp - jnp.max(lbp, axis=0, keepdims=True))
    lb_all = ex[0:1, :] / jnp.sum(ex, axis=0, keepdims=True)
    half_span = 0.5 * (1.0 - lb_all)
    nw_all = nw_ref[...]
    n_lv = len(LEVELS)

    def front(c):
        rows = pl.ds(c * CHUNK, CHUNK)
        qf_all = _silu(q_ref[rows, :].astype(F32))
        u = half_span * jnp.tanh(0.5 * f_ref[rows, :].astype(F32))
        f = (1.0 - half_span) + u
        kf_all = half_span - u
        lf = jnp.log2(jnp.maximum(f, 1e-30))
        hi = lf.astype(BF16)
        lo = (lf - hi.astype(F32)).astype(BF16)
        z = jnp.dot(zmat_ref[...], jnp.concatenate([hi, lo], axis=0), preferred_element_type=F32)
        e_all = jnp.exp2(z)
        gate_all = nw_all * _silu(g_ref[rows, :].astype(F32))
        return qf_all, kf_all, e_all, gate_all

    def back(c, prepared):
        rows = pl.ds(c * CHUNK, CHUNK)
        qf_all, kf_all, e_all, gate_all = prepared
        heads = [slice(h * HEAD, (h + 1) * HEAD) for h in range(hp)]
        n_grp = CHUNK // PACK
        vb = [i_ref[rows, hs] for hs in heads]
        inter = []
        for h, hs in enumerate(heads):
            st = st_ref[h]
            e_b = e_all[0:CHUNK, hs]
            inter.append(lax.dot_general((qf_all[:, hs] * e_b).astype(BF16), st.astype(BF16), NT_DIMS,
                                         preferred_element_type=F32))
            k_end = (kf_all[:, hs] * e_all[CHUNK:2 * CHUNK, hs]).astype(BF16)
            st_ref[h] = st * e_b[CHUNK - 1:CHUNK, :] + lax.dot_general(vb[h], k_end, TN_DIMS,
                                                                       preferred_element_type=F32)
        acc = [[None] * n_grp for _ in heads]
        q_b = [qf_all[:, hs].astype(BF16) for hs in heads]
        k_b = [kf_all[:, hs].astype(BF16) for hs in heads]
        for l, hl in enumerate(LEVELS + (0,)):
            mask_l = mask_ref[:, l * CHUNK:(l + 1) * CHUNK]
            groups = list(range(n_grp))
            if hl >= PACK:
                groups = [gi for gi in groups if (gi * PACK // hl) % 2 == 1]
                take = lambda a, gs=groups: jnp.concatenate([a[gi * PACK:(gi + 1) * PACK] for gi in gs], axis=0)
                mask_l = take(mask_l)
            for h, hs in enumerate(heads):
                if hl == 0:
                    ql, kl = q_b[h], k_b[h]
                else:
                    e_l = e_all[(2 + l) * CHUNK:(3 + l) * CHUNK, hs].astype(BF16)
                    kl = k_b[h] * e_l
                    ql = take(q_b[h]) * take(e_l) if hl >= PACK else q_b[h] * e_l
                p = lax.dot_general(ql, kl, NT_DIMS, preferred_element_type=F32).astype(BF16) * mask_l
                for n, gi in enumerate(groups):
                    piece = p[n * PACK:(n + 1) * PACK]
                    acc[h][gi] = piece if acc[h][gi] is None else acc[h][gi] + piece
        outs = [inter[h] + jnp.dot(jnp.concatenate(acc[h], axis=0), vb[h], preferred_element_type=F32)
                for h in range(hp)]
        for h, hs in enumerate(heads):
            o = outs[h]
            o = o * lax.rsqrt(jnp.mean(o * o, axis=-1, keepdims=True) + EPS)
            o_ref[rows, hs] = (o * gate_all[:, hs]).astype(o_ref.dtype)

    n_chunks = tb // CHUNK
    prepared = front(0)
    for c in range(n_chunks):
        upcoming = front(c + 1) if c + 1 < n_chunks else None
        back(c, prepared)
        prepared = upcoming


def _hgrn2(proj, lb_param, norm_w, batch, seq, heads, col_blocks, hp=2, tb=512):
    m = proj.shape[0]
    w = hp * HEAD
    nt = seq // tb
    cq, cf, ci, cg = col_blocks
    zmat, masks = _hgrn2_tables()

    def sec(c0):
        return pl.BlockSpec((tb, w), lambda b, h, t: (b * nt + t, c0 + h))

    return pl.pallas_call(
        functools.partial(_hgrn2_kernel, hp=hp, tb=tb),
        grid=(batch, heads // hp, nt),
        in_specs=[pl.BlockSpec((lb_param.shape[0], w), lambda b, h, t: (0, h)),
                  pl.BlockSpec((1, w), lambda b, h, t: (0, h)),
                  pl.BlockSpec(zmat.shape, lambda b, h, t: (0, 0)),
                  pl.BlockSpec(masks.shape, lambda b, h, t: (0, 0)),
                  sec(cq), sec(cf), sec(ci), sec(cg)],
        out_specs=pl.BlockSpec((tb, w), lambda b, h, t: (b * nt + t, h)),
        out_shape=jax.ShapeDtypeStruct((m, heads * HEAD), BF16),
        scratch_shapes=[pltpu.VMEM((hp, HEAD, HEAD), F32)],
        compiler_params=_params(("parallel", "parallel", "arbitrary")),
        name="hgrn2",
    )(lb_param, norm_w.reshape(1, -1), zmat, masks, proj, proj, proj, proj)


def _gmlp_kernel(u_ref, v_ref, lnw_ref, lnb_ref, ws_ref, bs_ref, o_ref, *, groups):
    v = _gelu(v_ref[...].astype(F32))
    mu = jnp.mean(v, axis=-1, keepdims=True)
    d = v - mu
    var = jnp.mean(d * d, axis=-1, keepdims=True)
    vn = (d * lax.rsqrt(var + EPS) * lnw_ref[...] + lnb_ref[...]).astype(BF16)
    row = lax.broadcasted_iota(jnp.int32, (HEAD, HEAD), 0)
    col = lax.broadcasted_iota(jnp.int32, (HEAD, HEAD), 1)
    causal = col <= row
    bs = bs_ref[...]
    for g in range(groups):
        gs = slice(g * HEAD, (g + 1) * HEAD)
        w = jnp.where(causal, ws_ref[g], 0.0).astype(BF16)
        z = jnp.dot(w, vn[:, gs], preferred_element_type=F32) + bs[:, g:g + 1]
        o_ref[:, gs] = (_gelu(u_ref[:, gs].astype(F32)) * z).astype(o_ref.dtype)


def _gmlp(proj, ln_w, ln_b, w_s, b_s, col_u, col_v):
    m = proj.shape[0]
    groups = w_s.shape[0]
    w = groups * HEAD
    return pl.pallas_call(
        functools.partial(_gmlp_kernel, groups=groups),
        grid=(m // HEAD,),
        in_specs=[pl.BlockSpec((HEAD, w), lambda i: (i, col_u)),
                  pl.BlockSpec((HEAD, w), lambda i: (i, col_v)),
                  pl.BlockSpec((1, w), lambda i: (0, 0)),
                  pl.BlockSpec((1, w), lambda i: (0, 0)),
                  pl.BlockSpec((groups, HEAD, HEAD), lambda i: (0, 0, 0)),
                  pl.BlockSpec((HEAD, groups), lambda i: (0, 0))],
        out_specs=pl.BlockSpec((HEAD, w), lambda i: (i, 0)),
        out_shape=jax.ShapeDtypeStruct((m, w), BF16),
        compiler_params=_params(("parallel",)),
        name="gmlp",
    )(proj, proj, ln_w.reshape(1, w), ln_b.reshape(1, w), w_s, b_s.T)


def _layer(x2d, p2d, batch, seq, pre_mix_w, w_in, lb_param, a_norm_w, gmlp_ln_w, gmlp_ln_b, w_spatial,
           b_spatial, w_out, post_mix_w, pre_ffn_w, w_gate, w_up, w_down, post_ffn_w, w_ple, w_ple_gate,
           post_ple_w):
    a_width = a_norm_w.shape[0]
    b_width = gmlp_ln_w.shape[0]
    heads = a_width // HEAD
    assert w_in.shape[1] == 4 * a_width + 2 * b_width and a_width == b_width
    hp = 8
    sec = a_width // (hp * HEAD)

    d = x2d.shape[1]
    proj, _, (wo, wg) = _skew_matmul(
        "proj", [x2d], [pre_mix_w], _prep_norm, 0, [w_in.astype(BF16)], [], _main_dot, w_in.shape[1],
        (w_out, w_gate), tm=1024, tn=1024, n_slices=8)
    a_out = _hgrn2(proj, lb_param, a_norm_w, batch, seq, heads, (0, sec, 2 * sec, 3 * sec), hp=hp)
    b_out = _gmlp(proj, gmlp_ln_w, gmlp_ln_b, w_spatial, b_spatial, 4, 5)
    mix, (wu,) = _matmul_cat(a_out, b_out, wo, (w_up,))
    act, (x1,), (wd, wpg) = _skew_matmul(
        "ffn_up", [x2d, mix], [post_mix_w, pre_ffn_w], _prep_resid_norm, 1, [wg, wu], [], _main_swiglu,
        w_gate.shape[1], (w_down, w_ple_gate), tm=1024, tn=512, n_slices=16)
    ff = _matmul(act, wd, tm=512, tn=512)
    y, (x2,), _ = _skew_matmul(
        "ple", [x1, ff], [post_ffn_w], _prep_resid, 1, [wpg, w_ple.astype(BF16)], [p2d.astype(BF16)],
        _main_gated_embedding, d, (), tm=1024, tn=512, n_slices=8)
    return _resid(x2, y, post_ple_w)


def kernel(x, p, pre_mix_w, w_in, lb_param, a_norm_w, gmlp_ln_w, gmlp_ln_b, w_spatial, b_spatial, w_out,
           post_mix_w, pre_ffn_w, w_gate, w_up, w_down, post_ffn_w, w_ple, w_ple_gate, post_ple_w):
    batch, seq, d = x.shape
    depth = p.shape[0]
    assert depth == 1 and lb_param.shape[0] == depth + 1
    x2d = x.reshape(batch * seq, d)
    for l in range(depth):
        x2d = _layer(x2d, p[l].reshape(batch * seq, -1), batch, seq, pre_mix_w[l], w_in[l], lb_param,
                     a_norm_w[l], gmlp_ln_w[l], gmlp_ln_b[l], w_spatial[l], b_spatial[l], w_out[l],
                     post_mix_w[l], pre_ffn_w[l], w_gate[l], w_up[l], w_down[l], post_ffn_w[l], w_ple[l],
                     w_ple_gate[l], post_ple_w[l])
    return x2d.reshape(batch, seq, d)
```

```python
import functools
import math

import jax
import jax.numpy as jnp
import numpy as np
from jax import lax
from jax.experimental import pallas as pl
from jax.experimental.pallas import tpu as pltpu

F32 = jnp.float32
BF16 = jnp.bfloat16
EPS = 1e-6
HEAD = 128
CHUNK = 128
LEVELS = (64, 32, 16, 8, 4, 2, 1)
PACK = 16
MXU_COLS = 256
VMEM_LIMIT = 56 * 1024 * 1024
NT_DIMS = (((1,), (1,)), ((), ()))
TN_DIMS = (((0,), (0,)), ((), ()))


def _params(sem):
    return pltpu.CompilerParams(dimension_semantics=sem, vmem_limit_bytes=VMEM_LIMIT)


def _sigmoid(x):
    return 0.5 * jnp.tanh(0.5 * x) + 0.5


def _silu(x):
    h = 0.5 * x
    return h + h * jnp.tanh(h)


def _gelu(x):
    return 0.5 * x * (1.0 + lax.erf(x * (1.0 / math.sqrt(2.0))))


def _rms(x, w):
    return x * lax.rsqrt(jnp.mean(x * x, axis=-1, keepdims=True) + EPS) * w


def _resid_kernel(x_ref, y_ref, w_ref, o_ref):
    o_ref[...] = x_ref[...] + _rms(y_ref[...].astype(F32), w_ref[...])


def _resid(x, y, w, tr=512):
    m, d = x.shape
    row = pl.BlockSpec((tr, d), lambda i: (i, 0))
    return pl.pallas_call(
        _resid_kernel,
        grid=(m // tr,),
        in_specs=[row, row, pl.BlockSpec((1, d), lambda i: (0, 0))],
        out_specs=row,
        out_shape=jax.ShapeDtypeStruct((m, d), F32),
        compiler_params=_params(("parallel",)),
        name="resid",
    )(x, y, w.reshape(1, d))


def _side_cast_specs(casts, ni, nj):
    specs, n_blocks = [], []
    for c in casts:
        rows = c.shape[0]
        br = next(b for b in range(PACK, rows + 1, PACK) if rows % b == 0 and rows // b <= ni * nj)
        nb = rows // br
        n_blocks.append(nb)
        specs.append(pl.BlockSpec((br, c.shape[1]), lambda i, j, nb=nb: (jnp.minimum(i * nj + j, nb - 1), 0)))
    return specs, tuple(n_blocks)


def _side_cast_step(src_refs, dst_refs, n_blocks):
    step = pl.program_id(0) * pl.num_programs(1) + pl.program_id(1)
    for s_ref, d_ref, nb in zip(src_refs, dst_refs, n_blocks):
        @pl.when(step < nb)
        def _(s_ref=s_ref, d_ref=d_ref):
            d_ref[...] = s_ref[...].astype(d_ref.dtype)


def _skew_kernel(*refs, n_prep, n_vec, n_w, n_extra, n_side, n_blocks, prep_fn, main_fn, nj, n_out,
                 n_slices, rs):
    n_cast = len(n_blocks)
    it = iter(refs)
    take = lambda n: [next(it) for _ in range(n)]
    prep_refs, vec_refs, w_refs, extra_refs, cast_src = (take(n_prep), take(n_vec), take(n_w), take(n_extra),
                                                         take(n_cast))
    (o_ref,), side_refs, cast_dst, (a_scr,) = take(1), take(n_side), take(n_cast), take(1)
    i = pl.program_id(0)
    j = pl.program_id(1)
    tn = o_ref.shape[1]

    def prepare():
        a_slice, side_vals = prep_fn([r[...] for r in prep_refs], [v[...] for v in vec_refs])
        r0 = pl.multiple_of(jnp.minimum(j, n_slices - 1) * rs, rs)
        a_scr[i % 2, pl.ds(r0, rs), :] = a_slice.astype(a_scr.dtype)
        for s_ref, val in zip(side_refs, side_vals):
            s_ref[...] = val

    @pl.when(i == 0)
    def _():
        prepare()

    def step(n_cols):
        cs = slice(0, n_cols)
        o_ref[:, cs] = main_fn(a_scr[(i + 1) % 2], w_refs, extra_refs, cs).astype(o_ref.dtype)
        prepare()

    n_live = -(-(n_out - (nj - 1) * tn) // MXU_COLS) * MXU_COLS
    if n_live == tn:
        @pl.when(i > 0)
        def _():
            step(tn)
    else:
        @pl.when((i > 0) & (j < nj - 1))
        def _():
            step(tn)

        @pl.when((i > 0) & (j == nj - 1))
        def _():
            step(n_live)

    _side_cast_step(cast_src, cast_dst, n_blocks)


def _prep_norm(vals, vecs):
    (x,), (w,) = vals, vecs
    return _rms(x, w), []


def _prep_resid_norm(vals, vecs):
    (x, y), (w, w_next) = vals, vecs
    x_new = x + _rms(y.astype(F32), w)
    return _rms(x_new, w_next), [x_new]


def _prep_resid(vals, vecs):
    (x, y), (w,) = vals, vecs
    x_new = x + _rms(y.astype(F32), w)
    return x_new, [x_new]


def _main_dot(a, w_refs, extra_refs, cs):
    return jnp.dot(a, w_refs[0][:, cs], preferred_element_type=F32)


def _main_swiglu(a, w_refs, extra_refs, cs):
    g = jnp.dot(a, w_refs[0][:, cs], preferred_element_type=F32)
    u = jnp.dot(a, w_refs[1][:, cs], preferred_element_type=F32)
    return _silu(g) * u


def _main_gated_embedding(a, w_refs, extra_refs, cs):
    gate = _sigmoid(jnp.dot(a, w_refs[0][:, cs], preferred_element_type=F32))
    emb = jnp.dot(extra_refs[0][...], w_refs[1][:, cs], preferred_element_type=F32)
    return emb * gate


def _skew_matmul(name, prep_ins, vecs, prep_fn, n_side, weights, extras, main_fn, n_out, casts, tm, tn,
                 n_slices):
    m, d = prep_ins[0].shape
    ni, nj = m // tm, pl.cdiv(n_out, tn)
    rs = tm // n_slices
    assert nj >= n_slices and tn % MXU_COLS == 0 and rs % PACK == 0
    last = ni * n_slices - 1
    slice_spec = pl.BlockSpec((rs, d), lambda i, j: (jnp.minimum(i * n_slices + jnp.minimum(j, n_slices - 1),
                                                                 last), 0))
    vec_spec = pl.BlockSpec((1, d), lambda i, j: (0, 0))
    col = lambda i, j: jnp.where(i == 0, 0, j)
    cast_specs, n_blocks = _side_cast_specs(casts, ni + 1, nj)
    outs = pl.pallas_call(
        functools.partial(_skew_kernel, n_prep=len(prep_ins), n_vec=len(vecs), n_w=len(weights),
                          n_extra=len(extras), n_side=n_side, n_blocks=n_blocks, prep_fn=prep_fn,
                          main_fn=main_fn, nj=nj, n_out=n_out, n_slices=n_slices, rs=rs),
        grid=(ni + 1, nj),
        in_specs=[slice_spec] * len(prep_ins) + [vec_spec] * len(vecs)
        + [pl.BlockSpec((w.shape[0], tn), lambda i, j: (0, col(i, j))) for w in weights]
        + [pl.BlockSpec((tm, e.shape[1]), lambda i, j: (jnp.maximum(i - 1, 0), 0)) for e in extras]
        + cast_specs,
        out_specs=[pl.BlockSpec((tm, tn), lambda i, j: (jnp.maximum(i - 1, 0), col(i, j)))]
        + [slice_spec] * n_side + cast_specs,
        out_shape=[jax.ShapeDtypeStruct((m, n_out), BF16)]
        + [jax.ShapeDtypeStruct((m, d), F32)] * n_side
        + [jax.ShapeDtypeStruct(c.shape, BF16) for c in casts],
        scratch_shapes=[pltpu.VMEM((2, tm, d), BF16)],
        compiler_params=_params(("arbitrary", "arbitrary")),
        name=name,
    )(*prep_ins, *[v.reshape(1, d) for v in vecs], *weights, *extras, *casts)
    return outs[0], outs[1:1 + n_side], outs[1 + n_side:]


def _mm2_kernel(a1_ref, a2_ref, w1_ref, w2_ref, *refs, n_blocks):
    n_cast = len(n_blocks)
    o_ref = refs[n_cast]
    acc = jnp.dot(a1_ref[...], w1_ref[...], preferred_element_type=F32)
    acc = acc + jnp.dot(a2_ref[...], w2_ref[...], preferred_element_type=F32)
    o_ref[...] = acc.astype(o_ref.dtype)
    _side_cast_step(refs[:n_cast], refs[n_cast + 1:], n_blocks)


def _matmul_cat(a1, a2, w, casts, tm=1024, tn=1024):
    m, k1 = a1.shape
    k2 = a2.shape[1]
    assert k1 == k2 and w.shape[0] == k1 + k2
    n = w.shape[1]
    cast_specs, n_blocks = _side_cast_specs(casts, m // tm, n // tn)
    outs = pl.pallas_call(
        functools.partial(_mm2_kernel, n_blocks=n_blocks),
        grid=(m // tm, n // tn),
        in_specs=[pl.BlockSpec((tm, k1), lambda i, j: (i, 0)),
                  pl.BlockSpec((tm, k2), lambda i, j: (i, 0)),
                  pl.BlockSpec((k1, tn), lambda i, j: (0, j)),
                  pl.BlockSpec((k2, tn), lambda i, j: (1, j))] + cast_specs,
        out_specs=[pl.BlockSpec((tm, tn), lambda i, j: (i, j))] + cast_specs,
        out_shape=[jax.ShapeDtypeStruct((m, n), BF16)] + [jax.ShapeDtypeStruct(c.shape, BF16) for c in casts],
        compiler_params=_params(("arbitrary", "arbitrary")),
        name="matmul_cat",
    )(a1, a2, w, w, *casts)
    return outs[0], outs[1:]


def _mm_kernel(a_ref, w_ref, o_ref):
    o_ref[...] = jnp.dot(a_ref[...], w_ref[...], preferred_element_type=F32).astype(o_ref.dtype)


def _matmul(a, w, tm, tn):
    m, k = a.shape
    n = w.shape[1]
    return pl.pallas_call(
        _mm_kernel,
        grid=(m // tm, n // tn),
        in_specs=[pl.BlockSpec((tm, k), lambda i, j: (i, 0)), pl.BlockSpec((k, tn), lambda i, j: (0, j))],
        out_specs=pl.BlockSpec((tm, tn), lambda i, j: (i, j)),
        out_shape=jax.ShapeDtypeStruct((m, n), BF16),
        compiler_params=_params(("parallel", "parallel")),
        name="matmul",
    )(a, w)


def _hgrn2_tables():
    t = np.arange(CHUNK)[:, None]
    s = np.arange(CHUNK)[None, :]
    mats = [s <= t, s > t]
    masks = []
    for h in LEVELS:
        blk = t // h
        odd = blk % 2 == 1
        ref = np.where(odd, blk * h - 1, blk * h + h - 1)
        if h > 1:
            mats.append(np.where(odd, (s > ref) & (s <= t), (s > t) & (s <= ref)))
        masks.append(odd & (s // h == blk - 1))
    masks.append(t == s)
    zmat = np.tile(np.concatenate(mats, axis=0).astype(np.float32), (1, 2))
    return jnp.asarray(zmat, BF16), jnp.asarray(np.concatenate(masks, axis=1).astype(np.float32), BF16)


def _hgrn2_kernel(lbp_ref, nw_ref, zmat_ref, mask_ref, q_ref, f_ref, i_ref, g_ref, o_ref, st_ref, *, hp, tb):
    @pl.when(pl.program_id(2) == 0)
    def _():
        st_ref[...] = jnp.zeros_like(st_ref)

    lbp = lbp_ref[...]
    ex = jnp.exp(lbp - jnp.max(lbp, axis=0, keepdims=True))
    lb_all = ex[0:1, :] / jnp.sum(ex, axis=0, keepdims=True)
    half_span = 0.5 * (1.0 - lb_all)
    nw_all = nw_ref[...]
    odd_row = lax.broadcasted_iota(jnp.int32, (CHUNK, hp * HEAD), 0) % 2 == 1

    def front(c):
        rows = pl.ds(c * CHUNK, CHUNK)
        qf_all = _silu(q_ref[rows, :].astype(F32))
        u = half_span * jnp.tanh(0.5 * f_ref[rows, :].astype(F32))
        f = jnp.maximum((1.0 - half_span) + u, 1e-30)
        kf_all = half_span - u
        lf = jnp.log2(f)
        hi = lf.astype(BF16)
        lo = (lf - hi.astype(F32)).astype(BF16)
        z = jnp.dot(zmat_ref[...], jnp.concatenate([hi, lo], axis=0), preferred_element_type=F32)
        e_head = jnp.exp2(z[0:2 * CHUNK])
        e_lv = jnp.concatenate([jnp.exp2(z[2 * CHUNK:]).astype(BF16),
                                jnp.where(odd_row, f, 1.0).astype(BF16)], axis=0)
        gate_all = nw_all * _silu(g_ref[rows, :].astype(F32))
        return qf_all, kf_all, e_head, e_lv, gate_all

    def back(c, prepared):
        rows = pl.ds(c * CHUNK, CHUNK)
        qf_all, kf_all, e_all, e_lv, gate_all = prepared
        heads = [slice(h * HEAD, (h + 1) * HEAD) for h in range(hp)]
        n_grp = CHUNK // PACK
        vb = [i_ref[rows, hs] for hs in heads]
        inter = []
        for h, hs in enumerate(heads):
            st = st_ref[h]
            e_b = e_all[0:CHUNK, hs]
            inter.append(lax.dot_general((qf_all[:, hs] * e_b).astype(BF16), st.astype(BF16), NT_DIMS,
                                         preferred_element_type=F32))
            k_end = (kf_all[:, hs] * e_all[CHUNK:2 * CHUNK, hs]).astype(BF16)
            st_ref[h] = st * e_b[CHUNK - 1:CHUNK, :] + lax.dot_general(vb[h], k_end, TN_DIMS,
                                                                       preferred_element_type=F32)
        acc = [[None] * n_grp for _ in heads]
        q_b = [qf_all[:, hs].astype(BF16) for hs in heads]
        k_b = [kf_all[:, hs].astype(BF16) for hs in heads]
        for l, hl in enumerate(LEVELS + (0,)):
            mask_l = mask_ref[:, l * CHUNK:(l + 1) * CHUNK]
            groups = list(range(n_grp))
            if hl >= PACK:
                groups = [gi for gi in groups if (gi * PACK // hl) % 2 == 1]
                take = lambda a, gs=groups: jnp.concatenate([a[gi * PACK:(gi + 1) * PACK] for gi in gs], axis=0)
                mask_l = take(mask_l)
            for h, hs in enumerate(heads):
                if hl == 0:
                    ql, kl = q_b[h], k_b[h]
                else:
                    e_l = e_lv[l * CHUNK:(l + 1) * CHUNK, hs]
                    kl = k_b[h] * e_l
                    ql = take(q_b[h]) * take(e_l) if hl >= PACK else q_b[h] * e_l
                p = lax.dot_general(ql, kl, NT_DIMS, preferred_element_type=F32).astype(BF16) * mask_l
                for n, gi in enumerate(groups):
                    piece = p[n * PACK:(n + 1) * PACK]
                    acc[h][gi] = piece if acc[h][gi] is None else acc[h][gi] + piece
        outs = [inter[h] + jnp.dot(jnp.concatenate(acc[h], axis=0), vb[h], preferred_element_type=F32)
                for h in range(hp)]
        for h, hs in enumerate(heads):
            o = outs[h]
            o = o * lax.rsqrt(jnp.mean(o * o, axis=-1, keepdims=True) + EPS)
            o_ref[rows, hs] = (o * gate_all[:, hs]).astype(o_ref.dtype)

    n_chunks = tb // CHUNK
    prepared = front(0)
    for c in range(n_chunks):
        upcoming = front(c + 1) if c + 1 < n_chunks else None
        back(c, prepared)
        prepared = upcoming


def _hgrn2(proj, lb_param, norm_w, batch, seq, heads, col_blocks, hp=2, tb=512):
    m = proj.shape[0]
    w = hp * HEAD
    nt = seq // tb
    cq, cf, ci, cg = col_blocks
    zmat, masks = _hgrn2_tables()

    def sec(c0):
        return pl.BlockSpec((tb, w), lambda b, h, t: (b * nt + t, c0 + h))

    return pl.pallas_call(
        functools.partial(_hgrn2_kernel, hp=hp, tb=tb),
        grid=(batch, heads // hp, nt),
        in_specs=[pl.BlockSpec((lb_param.shape[0], w), lambda b, h, t: (0, h)),
                  pl.BlockSpec((1, w), lambda b, h, t: (0, h)),
                  pl.BlockSpec(zmat.shape, lambda b, h, t: (0, 0)),
                  pl.BlockSpec(masks.shape, lambda b, h, t: (0, 0)),
                  sec(cq), sec(cf), sec(ci), sec(cg)],
        out_specs=pl.BlockSpec((tb, w), lambda b, h, t: (b * nt + t, h)),
        out_shape=jax.ShapeDtypeStruct((m, heads * HEAD), BF16),
        scratch_shapes=[pltpu.VMEM((hp, HEAD, HEAD), F32)],
        compiler_params=_params(("parallel", "parallel", "arbitrary")),
        name="hgrn2",
    )(lb_param, norm_w.reshape(1, -1), zmat, masks, proj, proj, proj, proj)


GMLP_CHUNKS = 2


def _gmlp_kernel(u_ref, v_ref, lnw_ref, lnb_ref, ws_ref, bs_ref, o_ref, *, groups):
    v = _gelu(v_ref[...].astype(F32))
    mu = jnp.mean(v, axis=-1, keepdims=True)
    d = v - mu
    var = jnp.mean(d * d, axis=-1, keepdims=True)
    vn = (d * lax.rsqrt(var + EPS) * lnw_ref[...] + lnb_ref[...]).astype(BF16)
    row = lax.broadcasted_iota(jnp.int32, (HEAD, HEAD), 0)
    col = lax.broadcasted_iota(jnp.int32, (HEAD, HEAD), 1)
    causal = col <= row
    bs = bs_ref[...]
    chunks = [slice(c * HEAD, (c + 1) * HEAD) for c in range(GMLP_CHUNKS)]
    for g in range(groups):
        gs = slice(g * HEAD, (g + 1) * HEAD)
        w = jnp.where(causal, ws_ref[g], 0.0).astype(BF16)
        z = jnp.dot(w, jnp.concatenate([vn[rs, gs] for rs in chunks], axis=1), preferred_element_type=F32)
        for c, rs in enumerate(chunks):
            zc = z[:, c * HEAD:(c + 1) * HEAD] + bs[:, g:g + 1]
            o_ref[rs, gs] = (_gelu(u_ref[rs, gs].astype(F32)) * zc).astype(o_ref.dtype)


def _gmlp(proj, ln_w, ln_b, w_s, b_s, col_u, col_v):
    m = proj.shape[0]
    groups = w_s.shape[0]
    w = groups * HEAD
    tr = GMLP_CHUNKS * HEAD
    return pl.pallas_call(
        functools.partial(_gmlp_kernel, groups=groups),
        grid=(m // tr,),
        in_specs=[pl.BlockSpec((tr, w), lambda i: (i, col_u)),
                  pl.BlockSpec((tr, w), lambda i: (i, col_v)),
                  pl.BlockSpec((1, w), lambda i: (0, 0)),
                  pl.BlockSpec((1, w), lambda i: (0, 0)),
                  pl.BlockSpec((groups, HEAD, HEAD), lambda i: (0, 0, 0)),
                  pl.BlockSpec((HEAD, groups), lambda i: (0, 0))],
        out_specs=pl.BlockSpec((tr, w), lambda i: (i, 0)),
        out_shape=jax.ShapeDtypeStruct((m, w), BF16),
        compiler_params=_params(("parallel",)),
        name="gmlp",
    )(proj, proj, ln_w.reshape(1, w), ln_b.reshape(1, w), w_s, b_s.T)


def _layer(x2d, p2d, batch, seq, pre_mix_w, w_in, lb_param, a_norm_w, gmlp_ln_w, gmlp_ln_b, w_spatial,
           b_spatial, w_out, post_mix_w, pre_ffn_w, w_gate, w_up, w_down, post_ffn_w, w_ple, w_ple_gate,
           post_ple_w):
    a_width = a_norm_w.shape[0]
    b_width = gmlp_ln_w.shape[0]
    heads = a_width // HEAD
    assert w_in.shape[1] == 4 * a_width + 2 * b_width and a_width == b_width
    hp = 16
    sec = a_width // (hp * HEAD)

    d = x2d.shape[1]
    proj, _, (wo, wg) = _skew_matmul(
        "proj", [x2d], [pre_mix_w], _prep_norm, 0, [w_in.astype(BF16)], [], _main_dot, w_in.shape[1],
        (w_out, w_gate), tm=1024, tn=1024, n_slices=8)
    a_out = _hgrn2(proj, lb_param, a_norm_w, batch, seq, heads, (0, sec, 2 * sec, 3 * sec), hp=hp)
    b_out = _gmlp(proj, gmlp_ln_w, gmlp_ln_b, w_spatial, b_spatial, 4, 5)
    mix, (wu,) = _matmul_cat(a_out, b_out, wo, (w_up,))
    act, (x1,), (wd, wpg) = _skew_matmul(
        "ffn_up", [x2d, mix], [post_mix_w, pre_ffn_w], _prep_resid_norm, 1, [wg, wu], [], _main_swiglu,
        w_gate.shape[1], (w_down, w_ple_gate), tm=1024, tn=512, n_slices=16)
    ff = _matmul(act, wd, tm=512, tn=512)
    y, (x2,), _ = _skew_matmul(
        "ple", [x1, ff], [post_ffn_w], _prep_resid, 1, [wpg, w_ple.astype(BF16)], [p2d.astype(BF16)],
        _main_gated_embedding, d, (), tm=1024, tn=512, n_slices=8)
    return _resid(x2, y, post_ple_w)


def kernel(x, p, pre_mix_w, w_in, lb_param, a_norm_w, gmlp_ln_w, gmlp_ln_b, w_spatial, b_spatial, w_out,
           post_mix_w, pre_ffn_w, w_gate, w_up, w_down, post_ffn_w, w_ple, w_ple_gate, post_ple_w):
    batch, seq, d = x.shape
    depth = p.shape[0]
    assert depth == 1 and lb_param.shape[0] == depth + 1
    x2d = x.reshape(batch * seq, d)
    for l in range(depth):
        x2d = _layer(x2d, p[l].reshape(batch * seq, -1), batch, seq, pre_mix_w[l], w_in[l], lb_param,
                     a_norm_w[l], gmlp_ln_w[l], gmlp_ln_b[l], w_spatial[l], b_spatial[l], w_out[l],
                     post_mix_w[l], pre_ffn_w[l], w_gate[l], w_up[l], w_down[l], post_ffn_w[l], w_ple[l],
                     w_ple_gate[l], post_ple_w[l])
    return x2d.reshape(batch, seq, d)
```
